```python
import math
import jax, jax.numpy as jnp
from jax import lax
import numpy as np

D_MODEL = 4096
BATCH = 1
SEQ = 8192
DEPTH = 1

HEAD_DIM = 128
N_HEADS_TOTAL = D_MODEL // HEAD_DIM
N_HEADS_A = N_HEADS_TOTAL // 2
N_KV_A = N_HEADS_A // 4
N_HEADS_B = N_HEADS_TOTAL - N_HEADS_A
MIX_WIDTH = (N_HEADS_A + N_HEADS_B) * HEAD_DIM
Q_A = N_HEADS_A * HEAD_DIM
KV_A = N_KV_A * HEAD_DIM
QKV_B = N_HEADS_B * HEAD_DIM
IN_COLS = Q_A + 2 * KV_A + 3 * QKV_B
D_FF = 11008
GRID_W = 64
ROPE_THETA = 10000.0
Q_BLOCK = 128
DILATED_CONFIGS = ((128, 1), (512, 4), (2048, 16))
N_REL_BUCKETS = 32
REL_MAX_DIST = 1024
LN_EPS = 1e-5
RMS_EPS = 1e-6
DEEPNORM_ALPHA = (2.0 * DEPTH) ** 0.25
DEEPNORM_BETA = (8.0 * DEPTH) ** -0.25

kernel_name = "hymba_axial_gqa_dilated_macaron_deepnorm"


def layer_norm(x, g, b):
    xf = x.astype(jnp.float32)
    mu = jnp.mean(xf, axis=-1, keepdims=True)
    var = jnp.mean(jnp.square(xf - mu), axis=-1, keepdims=True)
    return ((xf - mu) * lax.rsqrt(var + LN_EPS) * g + b).astype(x.dtype)


def rms_norm(x, g):
    xf = x.astype(jnp.float32)
    return (xf * lax.rsqrt(jnp.mean(xf * xf, axis=-1, keepdims=True) + RMS_EPS) * g).astype(x.dtype)


def swiglu(x, w_gate, w_up, w_down):
    return (jax.nn.silu(x @ w_gate) * (x @ w_up)) @ w_down


def axial_rope_tables(seq_len):
    rows = seq_len // GRID_W
    row = jnp.repeat(jnp.arange(rows, dtype=jnp.float32), GRID_W)
    col = jnp.tile(jnp.arange(GRID_W, dtype=jnp.float32), rows)
    half = HEAD_DIM // 2
    inv_freq = ROPE_THETA ** (-jnp.arange(0, half, 2, dtype=jnp.float32) / half)
    ang = jnp.concatenate([row[:, None] * inv_freq, col[:, None] * inv_freq], axis=-1)
    return jnp.cos(ang), jnp.sin(ang)


def apply_rope(x, cos, sin):
    xf = x.astype(jnp.float32)
    x1, x2 = xf[..., 0::2], xf[..., 1::2]
    c = cos[None, :, None, :]
    s = sin[None, :, None, :]
    out = jnp.stack([x1 * c - x2 * s, x1 * s + x2 * c], axis=-1).reshape(x.shape)
    return out.astype(x.dtype)


def t5_bucket(rel):
    nb = N_REL_BUCKETS // 2
    max_exact = nb // 2
    ret = jnp.where(rel > 0, nb, 0)
    n = jnp.abs(rel)
    large = max_exact + (jnp.log(jnp.maximum(n, 1).astype(jnp.float32) / max_exact)
                         / math.log(REL_MAX_DIST / max_exact) * (nb - max_exact)).astype(jnp.int32)
    large = jnp.minimum(large, nb - 1)
    return ret + jnp.where(n < max_exact, n, large)


def global_axial_gqa(q, k, v):
    B, S = q.shape[0], q.shape[1]
    nb = S // Q_BLOCK
    rep = N_HEADS_A // N_KV_A
    scale = HEAD_DIM ** -0.5
    qb = q.reshape(B, nb, Q_BLOCK, N_KV_A, rep, HEAD_DIM).transpose(1, 0, 2, 3, 4, 5)

    def block(qi):
        logits = jnp.einsum('bqgrd,bkgd->bgrqk', qi, k).astype(jnp.float32) * scale
        p = jax.nn.softmax(logits, axis=-1).astype(v.dtype)
        return jnp.einsum('bgrqk,bkgd->bqgrd', p, v)

    out = lax.map(block, qb)
    return out.transpose(1, 0, 2, 3, 4, 5).reshape(B, S, Q_A)


def dilated_window_attention(q, k, v, rel_bias):
    B, S = q.shape[0], q.shape[1]
    nb = S // Q_BLOCK
    scale = HEAD_DIM ** -0.5
    qb = q.reshape(B, nb, Q_BLOCK, N_HEADS_B, HEAD_DIM).transpose(1, 0, 2, 3, 4)
    starts = jnp.arange(nb, dtype=jnp.int32) * Q_BLOCK
    offsets = [dil * jnp.arange(-(win // (2 * dil)), win // (2 * dil) + 1, dtype=jnp.int32)
               for win, dil in DILATED_CONFIGS]
    biases = [rel_bias[t5_bucket(off)].astype(jnp.float32).T for off in offsets]

    def block(args):
        qi, start = args
        pos = start + jnp.arange(Q_BLOCK, dtype=jnp.int32)
        outs, lses = [], []
        for off, bias in zip(offsets, biases):
            idx = pos[:, None] + off[None, :]
            valid = (idx >= 0) & (idx < S)
            idxc = jnp.clip(idx, 0, S - 1)
            kg = k[:, idxc]
            vg = v[:, idxc]
            logits = jnp.einsum('bqhd,bqkhd->bqhk', qi, kg).astype(jnp.float32) * scale + bias[None, None]
            logits = jnp.where(valid[None, :, None, :], logits, -jnp.inf)
            m = jnp.max(logits, axis=-1, keepdims=True)
            e = jnp.exp(logits - m)
            s = jnp.sum(e, axis=-1, keepdims=True)
            o = jnp.einsum('bqhk,bqkhd->bqhd', (e / s).astype(v.dtype), vg)
            outs.append(o.astype(jnp.float32))
            lses.append(m[..., 0] + jnp.log(s[..., 0]))
        w = jax.nn.softmax(jnp.stack(lses, axis=0), axis=0)
        return jnp.einsum('nbqh,nbqhd->bqhd', w, jnp.stack(outs, axis=0)).astype(q.dtype)

    out = lax.map(block, (qb, starts))
    return out.transpose(1, 0, 2, 3, 4).reshape(B, S, QKV_B)


def setup_inputs(seed: int = 0) -> dict:
    key = jax.random.key(seed)
    ks = jax.random.split(key, 20)
    f32 = jnp.float32

    def nrm(k, shape, scale):
        return jax.random.normal(k, shape, f32) * scale

    def gain(k, shape):
        return 1.0 + 0.02 * jax.random.normal(k, shape, f32)

    x = jax.random.normal(ks[0], (BATCH, SEQ, D_MODEL), f32)
    beta = DEEPNORM_BETA
    ffn1_w_gate = nrm(ks[1], (DEPTH, D_MODEL, D_FF), D_MODEL ** -0.5)
    ffn1_w_up = nrm(ks[2], (DEPTH, D_MODEL, D_FF), D_MODEL ** -0.5)
    ffn1_w_down = nrm(ks[3], (DEPTH, D_FF, D_MODEL), beta * D_FF ** -0.5)
    ln1_g = gain(ks[4], (DEPTH, D_MODEL))
    ln1_b = nrm(ks[5], (DEPTH, D_MODEL), 0.02)
    col_scale = jnp.concatenate([
        jnp.ones((Q_A + KV_A,), f32), jnp.full((KV_A,), beta, f32),
        jnp.ones((2 * QKV_B,), f32), jnp.full((QKV_B,), beta, f32)])
    w_in = nrm(ks[6], (DEPTH, D_MODEL, IN_COLS), D_MODEL ** -0.5) * col_scale
    q_norm_g = gain(ks[7], (DEPTH, HEAD_DIM))
    k_norm_g = gain(ks[8], (DEPTH, HEAD_DIM))
    rel_bias = nrm(ks[9], (N_REL_BUCKETS, N_HEADS_B), 0.5)
    w_out = nrm(ks[10], (DEPTH, MIX_WIDTH, D_MODEL), beta * MIX_WIDTH ** -0.5)
    ln2_g = gain(ks[11], (DEPTH, D_MODEL))
    ln2_b = nrm(ks[12], (DEPTH, D_MODEL), 0.02)
    ffn2_w_gate = nrm(ks[13], (DEPTH, D_MODEL, D_FF), D_MODEL ** -0.5)
    ffn2_w_up = nrm(ks[14], (DEPTH, D_MODEL, D_FF), D_MODEL ** -0.5)
    ffn2_w_down = nrm(ks[15], (DEPTH, D_FF, D_MODEL), beta * D_FF ** -0.5)
    ln3_g = gain(ks[16], (DEPTH, D_MODEL))
    ln3_b = nrm(ks[17], (DEPTH, D_MODEL), 0.02)
    return {"x": x,
            "ffn1_w_gate": ffn1_w_gate, "ffn1_w_up": ffn1_w_up, "ffn1_w_down": ffn1_w_down,
            "ln1_g": ln1_g, "ln1_b": ln1_b,
            "w_in": w_in, "q_norm_g": q_norm_g, "k_norm_g": k_norm_g,
            "rel_bias": rel_bias, "w_out": w_out,
            "ln2_g": ln2_g, "ln2_b": ln2_b,
            "ffn2_w_gate": ffn2_w_gate, "ffn2_w_up": ffn2_w_up, "ffn2_w_down": ffn2_w_down,
            "ln3_g": ln3_g, "ln3_b": ln3_b}


def reference(x, ffn1_w_gate, ffn1_w_up, ffn1_w_down, ln1_g, ln1_b,
              w_in, q_norm_g, k_norm_g, rel_bias, w_out, ln2_g, ln2_b,
              ffn2_w_gate, ffn2_w_up, ffn2_w_down, ln3_g, ln3_b):
    B, S = x.shape[0], x.shape[1]
    cos, sin = axial_rope_tables(S)
    splits = [Q_A, Q_A + KV_A, Q_A + 2 * KV_A, Q_A + 2 * KV_A + QKV_B, Q_A + 2 * KV_A + 2 * QKV_B]
    h = x
    for l in range(DEPTH):
        h = layer_norm(DEEPNORM_ALPHA * h + 0.5 * swiglu(h, ffn1_w_gate[l], ffn1_w_up[l], ffn1_w_down[l]),
                       ln1_g[l], ln1_b[l])
        proj = h @ w_in[l]
        qa, ka, va, qb, kb, vb = jnp.split(proj, splits, axis=-1)
        qa = apply_rope(rms_norm(qa.reshape(B, S, N_HEADS_A, HEAD_DIM), q_norm_g[l]), cos, sin)
        ka = apply_rope(rms_norm(ka.reshape(B, S, N_KV_A, HEAD_DIM), k_norm_g[l]), cos, sin)
        va = va.reshape(B, S, N_KV_A, HEAD_DIM)
        out_a = global_axial_gqa(qa, ka, va)
        out_b = dilated_window_attention(qb.reshape(B, S, N_HEADS_B, HEAD_DIM),
                                         kb.reshape(B, S, N_HEADS_B, HEAD_DIM),
                                         vb.reshape(B, S, N_HEADS_B, HEAD_DIM), rel_bias)
        mix = jnp.concatenate([out_a, out_b], axis=-1) @ w_out[l]
        h = layer_norm(DEEPNORM_ALPHA * h + mix, ln2_g[l], ln2_b[l])
        h = layer_norm(DEEPNORM_ALPHA * h + 0.5 * swiglu(h, ffn2_w_gate[l], ffn2_w_up[l], ffn2_w_down[l]),
                       ln3_g[l], ln3_b[l])
    return h
```

```python
import functools
import math

import jax
import jax.numpy as jnp
import numpy as np
from jax import lax
from jax.experimental import pallas as pl
from jax.experimental.pallas import tpu as pltpu

F32 = jnp.float32
BF16 = jnp.bfloat16

HEAD_DIM = 128
GQA_REP = 4
GRID_W = 64
ROPE_THETA = 10000.0
DILATED_CONFIGS = ((128, 1), (512, 4), (2048, 16))
N_REL_BUCKETS = 32
REL_MAX_DIST = 1024
LN_EPS = 1e-5
RMS_EPS = 1e-6
MASKED_LOGIT = -1e30

V7X_LANES = 128
V7X_SCOPED_VMEM_LIMIT_BYTES = 60000 * 1024

FFN_ROWS = 512
FFN_COLS = 256
PROJ_ROWS = 512
PROJ_COLS = 512
OUT_K = 512
LN_CHUNK = 32
ATT_A_Q = 256
ATT_A_K = 512
ATT_B_BLK = 256
ATT_B_HALO = max(w for w, _ in DILATED_CONFIGS) // 2


def _compiler_params(semantics):
    return pltpu.CompilerParams(dimension_semantics=semantics,
                                vmem_limit_bytes=V7X_SCOPED_VMEM_LIMIT_BYTES)


def _residual_layer_norm(res_ref, acc_ref, g_ref, b_ref, alpha, branch_scale):
    rows_total = acc_ref.shape[0]
    gain = g_ref[...]
    bias = b_ref[...]

    def body(c, carry):
        rows = pl.ds(pl.multiple_of(c * LN_CHUNK, LN_CHUNK), LN_CHUNK)
        y = alpha * res_ref[rows, :] + branch_scale * acc_ref[rows, :]
        mu = jnp.mean(y, axis=-1, keepdims=True)
        yc = y - mu
        var = jnp.mean(yc * yc, axis=-1, keepdims=True)
        acc_ref[rows, :] = yc * lax.rsqrt(var + LN_EPS) * gain + bias
        return carry

    lax.fori_loop(0, rows_total // LN_CHUNK, body, 0)


def _ffn_ln_kernel(x_ref, wg_ref, wu_ref, wd_ref, g_ref, b_ref, o_ref, xb_ref, *, alpha):
    j = pl.program_id(1)

    @pl.when(j == 0)
    def _():
        xb_ref[...] = x_ref[...].astype(BF16)
        o_ref[...] = jnp.zeros_like(o_ref)

    xb = xb_ref[...]
    gate = jnp.dot(xb, wg_ref[...], preferred_element_type=F32)
    up = jnp.dot(xb, wu_ref[...], preferred_element_type=F32)
    hidden = (gate * jax.nn.sigmoid(gate) * up).astype(BF16)
    o_ref[...] += jnp.dot(hidden, wd_ref[...], preferred_element_type=F32)

    @pl.when(j == pl.num_programs(1) - 1)
    def _():
        _residual_layer_norm(x_ref, o_ref, g_ref, b_ref, alpha, 0.5)


def _ffn_ln(x, wg, wu, wd, ln_g, ln_b, alpha):
    s, d = x.shape
    f = wg.shape[1]
    tm = min(FFN_ROWS, s)
    tf = min(FFN_COLS, f)
    assert s % tm == 0 and f % tf == 0 and tm % LN_CHUNK == 0
    return pl.pallas_call(
        functools.partial(_ffn_ln_kernel, alpha=alpha),
        name="ffn_ln",
        grid=(s // tm, f // tf),
        in_specs=[
            pl.BlockSpec((tm, d), lambda i, j: (i, 0)),
            pl.BlockSpec((d, tf), lambda i, j: (0, j)),
            pl.BlockSpec((d, tf), lambda i, j: (0, j)),
            pl.BlockSpec((tf, d), lambda i, j: (j, 0)),
            pl.BlockSpec((1, d), lambda i, j: (0, 0)),
            pl.BlockSpec((1, d), lambda i, j: (0, 0)),
        ],
        out_specs=pl.BlockSpec((tm, d), lambda i, j: (i, 0)),
        out_shape=jax.ShapeDtypeStruct((s, d), F32),
        scratch_shapes=[pltpu.VMEM((tm, d), BF16)],
        compiler_params=_compiler_params(("parallel", "arbitrary")),
    )(x, wg, wu, wd, ln_g, ln_b)


def _qkv_kernel(h_ref, w_ref, gain_ref, cos_ref, sin_ref, o_ref, hb_ref, *, n_rope_tiles):
    n = pl.program_id(1)

    @pl.when(n == 0)
    def _():
        hb_ref[...] = h_ref[...].astype(BF16)

    acc = jnp.dot(hb_ref[...], w_ref[...], preferred_element_type=F32)
    gain = gain_ref[...]

    @pl.when(n < n_rope_tiles)
    def _():
        cos = cos_ref[...]
        sin = sin_ref[...]
        lane = lax.broadcasted_iota(jnp.int32, cos.shape, 1)
        even = (lane & 1) == 0
        for hd in range(acc.shape[1] // HEAD_DIM):
            cols = slice(hd * HEAD_DIM, (hd + 1) * HEAD_DIM)
            xh = acc[:, cols]
            ms = jnp.mean(xh * xh, axis=-1, keepdims=True)
            xn = xh * lax.rsqrt(ms + RMS_EPS) * gain[:, cols]
            partner = jnp.where(even, pltpu.roll(xn, HEAD_DIM - 1, 1), pltpu.roll(xn, 1, 1))
            o_ref[:, cols] = (xn * cos + partner * sin).astype(BF16)

    @pl.when(n >= n_rope_tiles)
    def _():
        o_ref[...] = (acc * gain).astype(BF16)


def _qkv_projection(h, w_in, col_gain, cos_tab, sin_tab, n_rope_cols):
    s, d = h.shape
    n_cols = w_in.shape[1]
    tm = min(PROJ_ROWS, s)
    tn = PROJ_COLS
    assert s % tm == 0 and n_cols % tn == 0 and n_rope_cols % tn == 0
    return pl.pallas_call(
        functools.partial(_qkv_kernel, n_rope_tiles=n_rope_cols // tn),
        name="qkv_proj",
        grid=(s // tm, n_cols // tn),
        in_specs=[
            pl.BlockSpec((tm, d), lambda i, n: (i, 0)),
            pl.BlockSpec((d, tn), lambda i, n: (0, n)),
            pl.BlockSpec((1, tn), lambda i, n: (0, n)),
            pl.BlockSpec((tm, HEAD_DIM), lambda i, n: (i, 0)),
            pl.BlockSpec((tm, HEAD_DIM), lambda i, n: (i, 0)),
        ],
        out_specs=pl.BlockSpec((tm, tn), lambda i, n: (i, n)),
        out_shape=jax.ShapeDtypeStruct((s, n_cols), BF16),
        scratch_shapes=[pltpu.VMEM((tm, d), BF16)],
        compiler_params=_compiler_params(("parallel", "arbitrary")),
    )(h, w_in, col_gain, cos_tab, sin_tab)


def _attn_a_kernel(q_ref, k_ref, v_ref, o_ref, m_ref, l_ref, acc_ref, *, tk):
    n_kblocks = k_ref.shape[0] // tk
    m_ref[...] = jnp.full_like(m_ref, -jnp.inf)
    l_ref[...] = jnp.zeros_like(l_ref)
    acc_ref[...] = jnp.zeros_like(acc_ref)

    def body(kb, carry):
        rows = pl.ds(pl.multiple_of(kb * tk, tk), tk)
        k = k_ref[rows, :]
        v = v_ref[rows, :]
        for r in range(GQA_REP):
            q = q_ref[:, r * HEAD_DIM:(r + 1) * HEAD_DIM]
            s = lax.dot_general(q, k, (((1,), (1,)), ((), ())), preferred_element_type=F32)
            m_prev = m_ref[r]
            m_new = jnp.maximum(m_prev, jnp.max(s, axis=-1, keepdims=True))
            p = jnp.exp(s - m_new)
            rescale = jnp.exp(m_prev - m_new)
            l_ref[r] = rescale * l_ref[r] + jnp.sum(p, axis=-1, keepdims=True)
            acc_ref[r] = rescale * acc_ref[r] + jnp.dot(p.astype(BF16), v, preferred_element_type=F32)
            m_ref[r] = m_new
        return carry

    lax.fori_loop(0, n_kblocks, body, 0)
    for r in range(GQA_REP):
        o_ref[:, r * HEAD_DIM:(r + 1) * HEAD_DIM] = (acc_ref[r] / l_ref[r]).astype(BF16)


def _attention_a(qkv, n_groups, k_col0, v_col0):
    s = qkv.shape[0]
    tq = min(ATT_A_Q, s)
    tk = min(ATT_A_K, s)
    assert s % tq == 0 and s % tk == 0
    group_w = GQA_REP * HEAD_DIM
    k_blk0 = k_col0 // HEAD_DIM
    v_blk0 = v_col0 // HEAD_DIM
    return pl.pallas_call(
        functools.partial(_attn_a_kernel, tk=tk),
        name="attn_a",
        grid=(n_groups, s // tq),
        in_specs=[
            pl.BlockSpec((tq, group_w), lambda g, i: (i, g)),
            pl.BlockSpec((s, HEAD_DIM), lambda g, i: (0, k_blk0 + g)),
            pl.BlockSpec((s, HEAD_DIM), lambda g, i: (0, v_blk0 + g)),
        ],
        out_specs=pl.BlockSpec((tq, group_w), lambda g, i: (i, g)),
        out_shape=jax.ShapeDtypeStruct((s, n_groups * group_w), BF16),
        scratch_shapes=[pltpu.VMEM((GQA_REP, tq, 1), F32),
                        pltpu.VMEM((GQA_REP, tq, 1), F32),
                        pltpu.VMEM((GQA_REP, tq, HEAD_DIM), F32)],
        compiler_params=_compiler_params(("parallel", "arbitrary")),
    )(qkv, qkv, qkv)


def _attn_b_kernel(q_ref, k_ref, v_ref, t_ref, o_ref, s_ref, *, blk, n_side):
    i = pl.program_id(1)
    n_blocks = k_ref.shape[0] // blk
    q = q_ref[...]
    starts = []
    for jj in range(2 * n_side + 1):
        kb = i + (jj - n_side)
        valid = jnp.logical_and(kb >= 0, kb < n_blocks)
        start = pl.multiple_of(jnp.clip(kb, 0, n_blocks - 1) * blk, blk)
        starts.append(start)
        k = k_ref[pl.ds(start, blk), :]
        s = lax.dot_general(q, k, (((1,), (1,)), ((), ())), preferred_element_type=F32)
        s = s + t_ref[0, jj]
        s_ref[:, jj * blk:(jj + 1) * blk] = jnp.where(valid, s, MASKED_LOGIT)

    s_all = s_ref[...]
    m = jnp.max(s_all, axis=-1, keepdims=True)
    p = jnp.exp(s_all - m)
    denom = jnp.sum(p, axis=-1, keepdims=True)
    p = p.astype(BF16)
    acc = jnp.zeros((blk, HEAD_DIM), F32)
    for jj in range(2 * n_side + 1):
        v = v_ref[pl.ds(starts[jj], blk), :]
        acc = acc + jnp.dot(p[:, jj * blk:(jj + 1) * blk], v, preferred_element_type=F32)
    o_ref[...] = (acc / denom).astype(BF16)


def _attention_b(qkv, bias_tiles, n_heads, q_col0, k_col0, v_col0):
    s = qkv.shape[0]
    blk = ATT_B_BLK
    n_side = ATT_B_HALO // blk
    n_tiles = 2 * n_side + 1
    assert s % blk == 0 and ATT_B_HALO % blk == 0
    q_blk0, k_blk0, v_blk0 = (c // HEAD_DIM for c in (q_col0, k_col0, v_col0))
    return pl.pallas_call(
        functools.partial(_attn_b_kernel, blk=blk, n_side=n_side),
        name="attn_b",
        grid=(n_heads, s // blk),
        in_specs=[
            pl.BlockSpec((blk, HEAD_DIM), lambda h, i: (i, q_blk0 + h)),
            pl.BlockSpec((s, HEAD_DIM), lambda h, i: (0, k_blk0 + h)),
            pl.BlockSpec((s, HEAD_DIM), lambda h, i: (0, v_blk0 + h)),
            pl.BlockSpec((1, n_tiles, blk, blk), lambda h, i: (h, 0, 0, 0)),
        ],
        out_specs=pl.BlockSpec((blk, HEAD_DIM), lambda h, i: (i, h)),
        out_shape=jax.ShapeDtypeStruct((s, n_heads * HEAD_DIM), BF16),
        scratch_shapes=[pltpu.VMEM((blk, n_tiles * blk), F32)],
        compiler_params=_compiler_params(("parallel", "arbitrary")),
    )(qkv, qkv, qkv, bias_tiles)


def _out_proj_ln_kernel(a_ref, b_ref, w_ref, h_ref, g_ref, beta_ref, o_ref, *, alpha, n_a_chunks):
    k = pl.program_id(1)

    @pl.when(k == 0)
    def _():
        o_ref[...] = jnp.zeros_like(o_ref)

    @pl.when(k < n_a_chunks)
    def _():
        o_ref[...] += jnp.dot(a_ref[...], w_ref[...], preferred_element_type=F32)

    @pl.when(k >= n_a_chunks)
    def _():
        o_ref[...] += jnp.dot(b_ref[...], w_ref[...], preferred_element_type=F32)

    @pl.when(k == pl.num_programs(1) - 1)
    def _():
        _residual_layer_norm(h_ref, o_ref, g_ref, beta_ref, alpha, 1.0)


def _out_proj_ln(out_a, out_b, w_out, h, ln_g, ln_b, alpha):
    s, d = h.shape
    wa, wb = out_a.shape[1], out_b.shape[1]
    tm = min(PROJ_ROWS, s)
    tk = OUT_K
    assert s % tm == 0 and wa % tk == 0 and wb % tk == 0 and tm % LN_CHUNK == 0
    na, nb = wa // tk, wb // tk
    return pl.pallas_call(
        functools.partial(_out_proj_ln_kernel, alpha=alpha, n_a_chunks=na),
        name="out_proj_ln",
        grid=(s // tm, na + nb),
        in_specs=[
            pl.BlockSpec((tm, tk), lambda i, k: (i, jnp.minimum(k, na - 1))),
            pl.BlockSpec((tm, tk), lambda i, k: (i, jnp.maximum(k - na, 0))),
            pl.BlockSpec((tk, d), lambda i, k: (k, 0)),
            pl.BlockSpec((tm, d), lambda i, k: (i, 0)),
            pl.BlockSpec((1, d), lambda i, k: (0, 0)),
            pl.BlockSpec((1, d), lambda i, k: (0, 0)),
        ],
        out_specs=pl.BlockSpec((tm, d), lambda i, k: (i, 0)),
        out_shape=jax.ShapeDtypeStruct((s, d), F32),
        compiler_params=_compiler_params(("parallel", "arbitrary")),
    )(out_a, out_b, w_out, h, ln_g, ln_b)


def _rope_tables(seq_len):
    rows = seq_len // GRID_W
    row = jnp.repeat(jnp.arange(rows, dtype=F32), GRID_W)
    col = jnp.tile(jnp.arange(GRID_W, dtype=F32), rows)
    half = HEAD_DIM // 2
    inv_freq = ROPE_THETA ** (-jnp.arange(0, half, 2, dtype=F32) / half)
    ang = jnp.concatenate([row[:, None] * inv_freq, col[:, None] * inv_freq], axis=-1)
    cos = jnp.repeat(jnp.cos(ang), 2, axis=-1)
    sign = jnp.tile(jnp.array([-1.0, 1.0], F32), HEAD_DIM // 2)
    sin = jnp.repeat(jnp.sin(ang), 2, axis=-1) * sign
    return cos, sin


def _t5_bucket(rel):
    nb = N_REL_BUCKETS // 2
    max_exact = nb // 2
    ret = jnp.where(rel > 0, nb, 0)
    n = jnp.abs(rel)
    large = max_exact + (jnp.log(jnp.maximum(n, 1).astype(F32) / max_exact)
                         / math.log(REL_MAX_DIST / max_exact) * (nb - max_exact)).astype(jnp.int32)
    large = jnp.minimum(large, nb - 1)
    return ret + jnp.where(n < max_exact, n, large)


def _dilated_bias_tiles(rel_bias, blk):
    n_side = ATT_B_HALO // blk
    n_tiles = 2 * n_side + 1
    reach = ATT_B_HALO + blk
    off = np.arange(-reach + 1, reach)
    mult = np.zeros(off.shape, np.float64)
    for win, dil in DILATED_CONFIGS:
        mult += (np.abs(off) <= win // 2) & (off % dil == 0)
    log_mult = jnp.asarray(np.log(np.maximum(mult, 1.0)), F32)
    present = jnp.asarray(mult > 0)
    bias = rel_bias[_t5_bucket(jnp.asarray(off, jnp.int32))].astype(F32)
    term = jnp.where(present[:, None], bias + log_mult[:, None], MASKED_LOGIT).T
    n_heads = term.shape[0]
    length = term.shape[1]
    padded = jnp.concatenate([term, jnp.zeros((n_heads, 1), F32)], axis=1)
    skew = jnp.tile(padded, (1, blk))[:, :blk * length].reshape(n_heads, blk, length)
    band = skew[:, :, blk - 1:blk - 1 + n_tiles * blk]
    return band.reshape(n_heads, blk, n_tiles, blk).transpose(0, 2, 1, 3)


def kernel(x, ffn1_w_gate, ffn1_w_up, ffn1_w_down, ln1_g, ln1_b, w_in, q_norm_g, k_norm_g, rel_bias, w_out,
           ln2_g, ln2_b, ffn2_w_gate, ffn2_w_up, ffn2_w_down, ln3_g, ln3_b):
    batch, s, d = x.shape
    depth = ffn1_w_gate.shape[0]
    assert batch == 1
    n_heads = d // HEAD_DIM
    heads_a = n_heads // 2
    kv_a = heads_a // GQA_REP
    heads_b = n_heads - heads_a
    q_a, kv_a_w, qkv_b = heads_a * HEAD_DIM, kv_a * HEAD_DIM, heads_b * HEAD_DIM
    col_ka, col_va = q_a, q_a + kv_a_w
    col_qb = q_a + 2 * kv_a_w
    col_kb, col_vb = col_qb + qkv_b, col_qb + 2 * qkv_b
    alpha = (2.0 * depth) ** 0.25
    scale = HEAD_DIM ** -0.5

    cos_tab, sin_tab = _rope_tables(s)
    bias_tiles = _dilated_bias_tiles(rel_bias, ATT_B_BLK)

    h = x[0]
    for l in range(depth):
        h = _ffn_ln(h, ffn1_w_gate[l].astype(BF16), ffn1_w_up[l].astype(BF16), ffn1_w_down[l].astype(BF16),
                    ln1_g[l][None], ln1_b[l][None], alpha)
        col_gain = jnp.concatenate([
            jnp.tile(q_norm_g[l], heads_a) * scale, jnp.tile(k_norm_g[l], kv_a), jnp.ones((kv_a_w,), F32),
            jnp.full((qkv_b,), scale, F32), jnp.ones((2 * qkv_b,), F32)])[None]
        qkv = _qkv_projection(h, w_in[l].astype(BF16), col_gain, cos_tab, sin_tab, col_va)
        out_a = _attention_a(qkv, kv_a, col_ka, col_va)
        out_b = _attention_b(qkv, bias_tiles, heads_b, col_qb, col_kb, col_vb)
        h = _out_proj_ln(out_a, out_b, w_out[l].astype(BF16), h, ln2_g[l][None], ln2_b[l][None], alpha)
        h = _ffn_ln(h, ffn2_w_gate[l].astype(BF16), ffn2_w_up[l].astype(BF16), ffn2_w_down[l].astype(BF16),
                    ln3_g[l][None], ln3_b[l][None], alpha)
    return h[None]
```

```python
import functools
import math

import jax
import jax.numpy as jnp
import numpy as np
from jax import lax
from jax.experimental import pallas as pl
from jax.experimental.pallas import tpu as pltpu

F32 = jnp.float32
BF16 = jnp.bfloat16

HEAD_DIM = 128
GQA_REP = 4
GRID_W = 64
ROPE_THETA = 10000.0
DILATED_CONFIGS = ((128, 1), (512, 4), (2048, 16))
N_REL_BUCKETS = 32
REL_MAX_DIST = 1024
LN_EPS = 1e-5
RMS_EPS = 1e-6
MASKED_LOGIT = -1e30
LOG2_E = math.log2(math.e)

V7X_LANES = 128
V7X_SCOPED_VMEM_LIMIT_BYTES = 60000 * 1024

FFN_ROWS = 512
FFN_COLS = 256
PROJ_ROWS = 512
PROJ_COLS = 512
OUT_K = 512
LN_CHUNK = 32
ATT_A_Q = 512
ATT_A_K = 512
ATT_B_BLK = 256
ATT_B_HALO = max(w for w, _ in DILATED_CONFIGS) // 2


def _compiler_params(semantics):
    return pltpu.CompilerParams(dimension_semantics=semantics,
                                vmem_limit_bytes=V7X_SCOPED_VMEM_LIMIT_BYTES)


def _layer_norm_in_place(acc_ref, g_ref, b_ref):
    rows_total = acc_ref.shape[0]
    gain = g_ref[...]
    bias = b_ref[...]

    def body(c, carry):
        rows = pl.ds(pl.multiple_of(c * LN_CHUNK, LN_CHUNK), LN_CHUNK)
        y = acc_ref[rows, :]
        mu = jnp.mean(y, axis=-1, keepdims=True)
        yc = y - mu
        var = jnp.mean(yc * yc, axis=-1, keepdims=True)
        acc_ref[rows, :] = yc * lax.rsqrt(var + LN_EPS) * gain + bias
        return carry

    lax.fori_loop(0, rows_total // LN_CHUNK, body, 0)


def _ffn_ln_kernel(x_ref, wg_ref, wu_ref, wd_ref, g_ref, b_ref, o_ref, xb_ref, *, alpha):
    j = pl.program_id(1)

    @pl.when(j == 0)
    def _():
        x = x_ref[...]
        xb_ref[...] = x.astype(BF16)
        o_ref[...] = alpha * x

    xb = xb_ref[...]
    gate = jnp.dot(xb, wg_ref[...], preferred_element_type=F32)
    up = jnp.dot(xb, wu_ref[...], preferred_element_type=F32)
    hidden = (gate * jax.nn.sigmoid(gate) * up).astype(BF16)
    o_ref[...] += jnp.dot(hidden, wd_ref[...], preferred_element_type=F32)

    @pl.when(j == pl.num_programs(1) - 1)
    def _():
        _layer_norm_in_place(o_ref, g_ref, b_ref)


def _ffn_ln(x, wg, wu, wd, ln_g, ln_b, alpha):
    s, d = x.shape
    f = wg.shape[1]
    tm = min(FFN_ROWS, s)
    tf = min(FFN_COLS, f)
    assert s % tm == 0 and f % tf == 0 and tm % LN_CHUNK == 0
    return pl.pallas_call(
        functools.partial(_ffn_ln_kernel, alpha=alpha),
        name="ffn_ln",
        grid=(s // tm, f // tf),
        in_specs=[
            pl.BlockSpec((tm, d), lambda i, j: (i, 0)),
            pl.BlockSpec((d, tf), lambda i, j: (0, j)),
            pl.BlockSpec((d, tf), lambda i, j: (0, j)),
            pl.BlockSpec((tf, d), lambda i, j: (j, 0)),
            pl.BlockSpec((1, d), lambda i, j: (0, 0)),
            pl.BlockSpec((1, d), lambda i, j: (0, 0)),
        ],
        out_specs=pl.BlockSpec((tm, d), lambda i, j: (i, 0)),
        out_shape=jax.ShapeDtypeStruct((s, d), F32),
        scratch_shapes=[pltpu.VMEM((tm, d), BF16)],
        compiler_params=_compiler_params(("parallel", "arbitrary")),
    )(x, wg, wu, wd, ln_g, ln_b)


def _qkv_kernel(h_ref, w_ref, gain_ref, cos_ref, sin_ref, o_ref, hb_ref, *, n_rope_tiles):
    n = pl.program_id(1)

    @pl.when(n == 0)
    def _():
        hb_ref[...] = h_ref[...].astype(BF16)

    acc = jnp.dot(hb_ref[...], w_ref[...], preferred_element_type=F32)
    gain = gain_ref[...]

    @pl.when(n < n_rope_tiles)
    def _():
        cos = cos_ref[...]
        sin = sin_ref[...]
        lane = lax.broadcasted_iota(jnp.int32, cos.shape, 1)
        even = (lane & 1) == 0
        for hd in range(acc.shape[1] // HEAD_DIM):
            cols = slice(hd * HEAD_DIM, (hd + 1) * HEAD_DIM)
            xh = acc[:, cols]
            ms = jnp.mean(xh * xh, axis=-1, keepdims=True)
            xn = xh * lax.rsqrt(ms + RMS_EPS) * gain[:, cols]
            partner = jnp.where(even, pltpu.roll(xn, HEAD_DIM - 1, 1), pltpu.roll(xn, 1, 1))
            o_ref[:, cols] = (xn * cos + partner * sin).astype(BF16)

    @pl.when(n >= n_rope_tiles)
    def _():
        o_ref[...] = (acc * gain).astype(BF16)


def _qkv_projection(h, w_in, col_gain, cos_tab, sin_tab, n_rope_cols):
    s, d = h.shape
    n_cols = w_in.shape[1]
    tm = min(PROJ_ROWS, s)
    tn = PROJ_COLS
    assert s % tm == 0 and n_cols % tn == 0 and n_rope_cols % tn == 0
    return pl.pallas_call(
        functools.partial(_qkv_kernel, n_rope_tiles=n_rope_cols // tn),
        name="qkv_proj",
        grid=(s // tm, n_cols // tn),
        in_specs=[
            pl.BlockSpec((tm, d), lambda i, n: (i, 0)),
            pl.BlockSpec((d, tn), lambda i, n: (0, n)),
            pl.BlockSpec((1, tn), lambda i, n: (0, n)),
            pl.BlockSpec((tm, HEAD_DIM), lambda i, n: (i, 0)),
            pl.BlockSpec((tm, HEAD_DIM), lambda i, n: (i, 0)),
        ],
        out_specs=pl.BlockSpec((tm, tn), lambda i, n: (i, n)),
        out_shape=jax.ShapeDtypeStruct((s, n_cols), BF16),
        scratch_shapes=[pltpu.VMEM((tm, d), BF16)],
        compiler_params=_compiler_params(("parallel", "arbitrary")),
    )(h, w_in, col_gain, cos_tab, sin_tab)


def _transpose_bf16(x):
    return x.astype(F32).T.astype(BF16)


def _store_transposed_blocks(src_ref, dst_ref):
    n_blocks, _, blk = dst_ref.shape

    def body(c, carry):
        rows = pl.ds(pl.multiple_of(c * blk, blk), blk)
        dst_ref[c] = _transpose_bf16(src_ref[rows, :])
        return carry

    lax.fori_loop(0, n_blocks, body, 0)


def _attn_a_kernel(q_ref, k_ref, v_ref, o_ref, vt_ref, qt_ref, s_ref, smax_ref, m_ref, l_ref, acc_ref, *, tk):
    n_kblocks = k_ref.shape[0] // tk

    @pl.when(pl.program_id(1) == 0)
    def _():
        _store_transposed_blocks(v_ref, vt_ref)

    for r in range(GQA_REP):
        qt_ref[r] = _transpose_bf16(q_ref[:, r * HEAD_DIM:(r + 1) * HEAD_DIM])
    m_ref[...] = jnp.full_like(m_ref, -jnp.inf)
    l_ref[...] = jnp.zeros_like(l_ref)
    acc_ref[...] = jnp.zeros_like(acc_ref)

    def scores(kb, slot, r):
        k = k_ref[pl.ds(pl.multiple_of(kb * tk, tk), tk), :]
        st = jnp.dot(k, qt_ref[r], preferred_element_type=F32)
        s_ref[slot, r] = st
        smax_ref[slot, r] = jnp.max(st, axis=0, keepdims=True)

    def fold(kb, slot, r):
        m_prev = m_ref[r]
        m_new = jnp.maximum(m_prev, smax_ref[slot, r])
        p = jnp.exp2(s_ref[slot, r] - m_new)
        rescale = jnp.exp2(m_prev - m_new)
        l_ref[r] = rescale * l_ref[r] + jnp.sum(p, axis=0, keepdims=True)
        acc_ref[r] = rescale * acc_ref[r] + jnp.dot(vt_ref[kb], p.astype(BF16), preferred_element_type=F32)
        m_ref[r] = m_new

    for r in range(GQA_REP):
        scores(0, 0, r)

    def body(t, carry):
        for slot in range(2):
            kb = 2 * t + slot
            nxt = jnp.minimum(kb + 1, n_kblocks - 1)
            for r in range(GQA_REP):
                scores(nxt, 1 - slot, r)
                fold(kb, slot, r)
        return carry

    lax.fori_loop(0, n_kblocks // 2, body, 0)
    for r in range(GQA_REP):
        o_ref[:, r * HEAD_DIM:(r + 1) * HEAD_DIM] = (acc_ref[r] / l_ref[r]).T.astype(BF16)


def _attention_a(qkv, n_groups, k_col0, v_col0):
    s = qkv.shape[0]
    tq = min(ATT_A_Q, s)
    tk = min(ATT_A_K, s)
    assert s % tq == 0 and s % (2 * tk) == 0
    group_w = GQA_REP * HEAD_DIM
    k_blk0 = k_col0 // HEAD_DIM
    v_blk0 = v_col0 // HEAD_DIM
    return pl.pallas_call(
        functools.partial(_attn_a_kernel, tk=tk),
        name="attn_a",
        grid=(n_groups, s // tq),
        in_specs=[
            pl.BlockSpec((tq, group_w), lambda g, i: (i, g)),
            pl.BlockSpec((s, HEAD_DIM), lambda g, i: (0, k_blk0 + g)),
            pl.BlockSpec((s, HEAD_DIM), lambda g, i: (0, v_blk0 + g)),
        ],
        out_specs=pl.BlockSpec((tq, group_w), lambda g, i: (i, g)),
        out_shape=jax.ShapeDtypeStruct((s, n_groups * group_w), BF16),
        scratch_shapes=[pltpu.VMEM((s // tk, HEAD_DIM, tk), BF16),
                        pltpu.VMEM((GQA_REP, HEAD_DIM, tq), BF16),
                        pltpu.VMEM((2, GQA_REP, tk, tq), F32),
                        pltpu.VMEM((2, GQA_REP, 1, tq), F32),
                        pltpu.VMEM((GQA_REP, 1, tq), F32),
                        pltpu.VMEM((GQA_REP, 1, tq), F32),
                        pltpu.VMEM((GQA_REP, HEAD_DIM, tq), F32)],
        compiler_params=_compiler_params(("parallel", "arbitrary")),
    )(qkv, qkv, qkv)


def _attn_b_kernel(q_ref, k_ref, v_ref, rev_ref, o_ref, vt_ref, t_ref, s_ref, smax_ref, *, blk, n_side):
    n_blocks = k_ref.shape[0] // blk
    n_tiles = 2 * n_side + 1
    _store_transposed_blocks(v_ref, vt_ref)

    for jj in range(n_tiles):
        start = (n_tiles - 1 - jj) * blk
        window = jnp.broadcast_to(rev_ref[0, :, start:start + 2 * blk], (blk, 2 * blk))
        t_ref[jj] = pltpu.roll(window, 0, 1, stride=1, stride_axis=0)[:, blk:]
    t_ref[n_tiles] = jnp.full((blk, blk), MASKED_LOGIT, F32)

    def band(i, jj):
        kb = i + (jj - n_side)
        valid = jnp.logical_and(kb >= 0, kb < n_blocks)
        return jnp.clip(kb, 0, n_blocks - 1), jnp.where(valid, jj, n_tiles)

    def q_transposed(i):
        return _transpose_bf16(q_ref[pl.ds(pl.multiple_of(i * blk, blk), blk), :])

    def scores_tile(i, slot, jj, qt, col_max):
        kb, tile = band(i, jj)
        k = k_ref[pl.ds(pl.multiple_of(kb * blk, blk), blk), :]
        s = jnp.dot(k, qt, preferred_element_type=F32) + t_ref[tile]
        s_ref[slot, jj] = s
        tile_max = jnp.max(s, axis=0, keepdims=True)
        return tile_max if col_max is None else jnp.maximum(col_max, tile_max)

    def fold_tile(i, slot, jj, m, acc, denom):
        kb, _ = band(i, jj)
        p = jnp.exp2(s_ref[slot, jj] - m)
        denom = denom + jnp.sum(p, axis=0, keepdims=True)
        acc = acc + jnp.dot(vt_ref[kb], p.astype(BF16), preferred_element_type=F32)
        return acc, denom

    qt = q_transposed(0)
    col_max = None
    for jj in range(n_tiles):
        col_max = scores_tile(0, 0, jj, qt, col_max)
    smax_ref[0] = col_max

    def body(t, carry):
        for slot in range(2):
            i = 2 * t + slot
            nxt = jnp.minimum(i + 1, n_blocks - 1)
            qt = q_transposed(nxt)
            m = smax_ref[slot]
            acc = jnp.zeros((HEAD_DIM, blk), F32)
            denom = jnp.zeros((1, blk), F32)
            col_max = None
            for jj in range(n_tiles):
                col_max = scores_tile(nxt, 1 - slot, jj, qt, col_max)
                acc, denom = fold_tile(i, slot, jj, m, acc, denom)
            smax_ref[1 - slot] = col_max
            o_ref[pl.ds(pl.multiple_of(i * blk, blk), blk), :] = (acc / denom).T.astype(BF16)
        return carry

    lax.fori_loop(0, n_blocks // 2, body, 0)


def _attention_b(qkv, offset_terms, n_heads, q_col0, k_col0, v_col0):
    s = qkv.shape[0]
    blk = ATT_B_BLK
    n_side = ATT_B_HALO // blk
    n_tiles = 2 * n_side + 1
    assert s % (2 * blk) == 0 and ATT_B_HALO % blk == 0
    q_blk0, k_blk0, v_blk0 = (c // HEAD_DIM for c in (q_col0, k_col0, v_col0))
    return pl.pallas_call(
        functools.partial(_attn_b_kernel, blk=blk, n_side=n_side),
        name="attn_b",
        grid=(n_heads,),
        in_specs=[
            pl.BlockSpec((s, HEAD_DIM), lambda h: (0, q_blk0 + h)),
            pl.BlockSpec((s, HEAD_DIM), lambda h: (0, k_blk0 + h)),
            pl.BlockSpec((s, HEAD_DIM), lambda h: (0, v_blk0 + h)),
            pl.BlockSpec((1, 1, (n_tiles + 1) * blk), lambda h: (h, 0, 0)),
        ],
        out_specs=pl.BlockSpec((s, HEAD_DIM), lambda h: (0, h)),
        out_shape=jax.ShapeDtypeStruct((s, n_heads * HEAD_DIM), BF16),
        scratch_shapes=[pltpu.VMEM((s // blk, HEAD_DIM, blk), BF16),
                        pltpu.VMEM((n_tiles + 1, blk, blk), F32),
                        pltpu.VMEM((2, n_tiles, blk, blk), F32),
                        pltpu.VMEM((2, 1, blk), F32)],
        compiler_params=_compiler_params(("parallel",)),
    )(qkv, qkv, qkv, offset_terms)


def _out_proj_ln_kernel(a_ref, b_ref, w_ref, h_ref, g_ref, beta_ref, o_ref, *, alpha, n_a_chunks):
    k = pl.program_id(1)

    @pl.when(k == 0)
    def _():
        o_ref[...] = alpha * h_ref[...]

    @pl.when(k < n_a_chunks)
    def _():
        o_ref[...] += jnp.dot(a_ref[...], w_ref[...], preferred_element_type=F32)

    @pl.when(k >= n_a_chunks)
    def _():
        o_ref[...] += jnp.dot(b_ref[...], w_ref[...], preferred_element_type=F32)

    @pl.when(k == pl.num_programs(1) - 1)
    def _():
        _layer_norm_in_place(o_ref, g_ref, beta_ref)


def _out_proj_ln(out_a, out_b, w_out, h, ln_g, ln_b, alpha):
    s, d = h.shape
    wa, wb = out_a.shape[1], out_b.shape[1]
    tm = min(PROJ_ROWS, s)
    tk = OUT_K
    assert s % tm == 0 and wa % tk == 0 and wb % tk == 0 and tm % LN_CHUNK == 0
    na, nb = wa // tk, wb // tk
    return pl.pallas_call(
        functools.partial(_out_proj_ln_kernel, alpha=alpha, n_a_chunks=na),
        name="out_proj_ln",
        grid=(s // tm, na + nb),
        in_specs=[
            pl.BlockSpec((tm, tk), lambda i, k: (i, jnp.minimum(k, na - 1))),
            pl.BlockSpec((tm, tk), lambda i, k: (i, jnp.maximum(k - na, 0))),
            pl.BlockSpec((tk, d), lambda i, k: (k, 0)),
            pl.BlockSpec((tm, d), lambda i, k: (i, 0)),
            pl.BlockSpec((1, d), lambda i, k: (0, 0)),
            pl.BlockSpec((1, d), lambda i, k: (0, 0)),
        ],
        out_specs=pl.BlockSpec((tm, d), lambda i, k: (i, 0)),
        out_shape=jax.ShapeDtypeStruct((s, d), F32),
        compiler_params=_compiler_params(("parallel", "arbitrary")),
    )(out_a, out_b, w_out, h, ln_g, ln_b)


def _rope_tables(seq_len):
    rows = seq_len // GRID_W
    row = jnp.repeat(jnp.arange(rows, dtype=F32), GRID_W)
    col = jnp.tile(jnp.arange(GRID_W, dtype=F32), rows)
    half = HEAD_DIM // 2
    inv_freq = ROPE_THETA ** (-jnp.arange(0, half, 2, dtype=F32) / half)
    ang = jnp.concatenate([row[:, None] * inv_freq, col[:, None] * inv_freq], axis=-1)
    cos = jnp.repeat(jnp.cos(ang), 2, axis=-1)
    sign = jnp.tile(jnp.array([-1.0, 1.0], F32), HEAD_DIM // 2)
    sin = jnp.repeat(jnp.sin(ang), 2, axis=-1) * sign
    return cos, sin


def _t5_bucket(rel):
    nb = N_REL_BUCKETS // 2
    max_exact = nb // 2
    ret = jnp.where(rel > 0, nb, 0)
    n = jnp.abs(rel)
    large = max_exact + (jnp.log(jnp.maximum(n, 1).astype(F32) / max_exact)
                         / math.log(REL_MAX_DIST / max_exact) * (nb - max_exact)).astype(jnp.int32)
    large = jnp.minimum(large, nb - 1)
    return ret + jnp.where(n < max_exact, n, large)


def _dilated_offset_terms(rel_bias, blk):
    reach = ATT_B_HALO + blk
    off = np.arange(-reach + 1, reach)
    mult = np.zeros(off.shape, np.float64)
    for win, dil in DILATED_CONFIGS:
        mult += (np.abs(off) <= win // 2) & (off % dil == 0)
    log_mult = jnp.asarray(np.log(np.maximum(mult, 1.0)), F32)
    present = jnp.asarray(mult > 0)
    bias = rel_bias[_t5_bucket(jnp.asarray(off, jnp.int32))].astype(F32)
    term = jnp.where(present[:, None], (bias + log_mult[:, None]) * LOG2_E, MASKED_LOGIT).T
    n_heads = term.shape[0]
    return jnp.concatenate([jnp.zeros((n_heads, 1), F32), term[:, ::-1]], axis=1)[:, None, :]


def kernel(x, ffn1_w_gate, ffn1_w_up, ffn1_w_down, ln1_g, ln1_b, w_in, q_norm_g, k_norm_g, rel_bias, w_out,
           ln2_g, ln2_b, ffn2_w_gate, ffn2_w_up, ffn2_w_down, ln3_g, ln3_b):
    batch, s, d = x.shape
    depth = ffn1_w_gate.shape[0]
    assert batch == 1
    n_heads = d // HEAD_DIM
    heads_a = n_heads // 2
    kv_a = heads_a // GQA_REP
    heads_b = n_heads - heads_a
    q_a, kv_a_w, qkv_b = heads_a * HEAD_DIM, kv_a * HEAD_DIM, heads_b * HEAD_DIM
    col_ka, col_va = q_a, q_a + kv_a_w
    col_qb = q_a + 2 * kv_a_w
    col_kb, col_vb = col_qb + qkv_b, col_qb + 2 * qkv_b
    alpha = (2.0 * depth) ** 0.25
    scale = HEAD_DIM ** -0.5 * LOG2_E

    cos_tab, sin_tab = _rope_tables(s)
    offset_terms = _dilated_offset_terms(rel_bias, ATT_B_BLK)

    h = x[0]
    for l in range(depth):
        h = _ffn_ln(h, ffn1_w_gate[l].astype(BF16), ffn1_w_up[l].astype(BF16), (0.5 * ffn1_w_down[l]).astype(BF16),
                    ln1_g[l][None], ln1_b[l][None], alpha)
        col_gain = jnp.concatenate([
            jnp.tile(q_norm_g[l], heads_a) * scale, jnp.tile(k_norm_g[l], kv_a), jnp.ones((kv_a_w,), F32),
            jnp.full((qkv_b,), scale, F32), jnp.ones((2 * qkv_b,), F32)])[None]
        qkv = _qkv_projection(h, w_in[l].astype(BF16), col_gain, cos_tab, sin_tab, col_va)
        out_a = _attention_a(qkv, kv_a, col_ka, col_va)
        out_b = _attention_b(qkv, offset_terms, heads_b, col_qb, col_kb, col_vb)
        h = _out_proj_ln(out_a, out_b, w_out[l].astype(BF16), h, ln2_g[l][None], ln2_b[l][None], alpha)
        h = _ffn_ln(h, ffn2_w_gate[l].astype(BF16), ffn2_w_up[l].astype(BF16), (0.5 * ffn2_w_down[l]).astype(BF16),
                    ln3_g[l][None], ln3_b[l][None], alpha)
    return h[None]
```

```python
import functools
import math

import jax
import jax.numpy as jnp
import numpy as np
from jax import lax
from jax.experimental import pallas as pl
from jax.experimental.pallas import tpu as pltpu

F32 = jnp.float32
BF16 = jnp.bfloat16

HEAD_DIM = 128
GQA_REP = 4
GRID_W = 64
ROPE_THETA = 10000.0
DILATED_CONFIGS = ((128, 1), (512, 4), (2048, 16))
N_REL_BUCKETS = 32
REL_MAX_DIST = 1024
LN_EPS = 1e-5
RMS_EPS = 1e-6
MASKED_LOGIT = -1e30
LOG2_E = math.log2(math.e)

V7X_LANES = 128
V7X_SCOPED_VMEM_LIMIT_BYTES = 60000 * 1024

FFN_ROWS = 1024
FFN_COLS = 256
QKV_ROWS = 1024
QKV_COLS = 512
OUT_ROWS = 512
OUT_K = 1024
LN_CHUNK = 128
ATT_A_Q = 512
ATT_A_K = 512
ATT_B_BLK = 256
ATT_B_HALO = max(w for w, _ in DILATED_CONFIGS) // 2


def _compiler_params(semantics):
    return pltpu.CompilerParams(dimension_semantics=semantics,
                                vmem_limit_bytes=V7X_SCOPED_VMEM_LIMIT_BYTES)


def _layer_norm_in_place(acc_ref, g_ref, b_ref):
    rows_total = acc_ref.shape[0]
    gain = g_ref[...]
    bias = b_ref[...]

    def body(c, carry):
        rows = pl.ds(pl.multiple_of(c * LN_CHUNK, LN_CHUNK), LN_CHUNK)
        y = acc_ref[rows, :]
        mu = jnp.mean(y, axis=-1, keepdims=True)
        yc = y - mu
        var = jnp.mean(yc * yc, axis=-1, keepdims=True)
        acc_ref[rows, :] = yc * lax.rsqrt(var + LN_EPS) * gain + bias
        return carry

    lax.fori_loop(0, rows_total // LN_CHUNK, body, 0)


def _ffn_ln_kernel(x_hbm, wg_ref, wu_ref, wd_ref, g_ref, b_ref, o_ref, xb_ref, x_sem, *, alpha):
    i = pl.program_id(0)
    j = pl.program_id(1)
    tm = o_ref.shape[0]

    @pl.when(j == 0)
    def _():
        rows = pl.ds(pl.multiple_of(i * tm, tm), tm)
        load_x = pltpu.make_async_copy(x_hbm.at[rows, :], o_ref, x_sem)
        load_x.start()
        load_x.wait()
        x = o_ref[...]
        xb_ref[...] = x.astype(BF16)
        o_ref[...] = alpha * x

    xb = xb_ref[...]
    gate = jnp.dot(xb, wg_ref[...].astype(BF16), preferred_element_type=F32)
    up = jnp.dot(xb, wu_ref[...].astype(BF16), preferred_element_type=F32)
    hidden = (0.5 * gate * jax.nn.sigmoid(gate) * up).astype(BF16)
    o_ref[...] += jnp.dot(hidden, wd_ref[...].astype(BF16), preferred_element_type=F32)

    @pl.when(j == pl.num_programs(1) - 1)
    def _():
        _layer_norm_in_place(o_ref, g_ref, b_ref)


def _ffn_ln(x, wg, wu, wd, layer, ln_g, ln_b, alpha):
    s, d = x.shape
    f = wg.shape[2]
    tm = min(FFN_ROWS, s)
    tf = min(FFN_COLS, f)
    assert s % tm == 0 and f % tf == 0 and tm % LN_CHUNK == 0
    return pl.pallas_call(
        functools.partial(_ffn_ln_kernel, alpha=alpha),
        name="ffn_ln",
        grid=(s // tm, f // tf),
        in_specs=[
            pl.BlockSpec(memory_space=pl.ANY),
            pl.BlockSpec((None, d, tf), lambda i, j: (layer, 0, j)),
            pl.BlockSpec((None, d, tf), lambda i, j: (layer, 0, j)),
            pl.BlockSpec((None, tf, d), lambda i, j: (layer, j, 0)),
            pl.BlockSpec((1, d), lambda i, j: (0, 0)),
            pl.BlockSpec((1, d), lambda i, j: (0, 0)),
        ],
        out_specs=pl.BlockSpec((tm, d), lambda i, j: (i, 0), pipeline_mode=pl.Buffered(1)),
        out_shape=jax.ShapeDtypeStruct((s, d), F32),
        scratch_shapes=[pltpu.VMEM((tm, d), BF16), pltpu.SemaphoreType.DMA(())],
        compiler_params=_compiler_params(("parallel", "arbitrary")),
    )(x, wg, wu, wd, ln_g, ln_b)


def _qkv_kernel(h_ref, w_ref, gain_ref, cos_ref, sin_ref, o_ref, hb_ref, *, n_rope_tiles):
    n = pl.program_id(1)

    @pl.when(n == 0)
    def _():
        hb_ref[...] = h_ref[...].astype(BF16)

    acc = jnp.dot(hb_ref[...], w_ref[...], preferred_element_type=F32)
    gain = gain_ref[...]

    @pl.when(n < n_rope_tiles)
    def _():
        cos = cos_ref[...]
        sin = sin_ref[...]
        lane = lax.broadcasted_iota(jnp.int32, cos.shape, 1)
        even = (lane & 1) == 0
        for hd in range(acc.shape[1] // HEAD_DIM):
            cols = slice(hd * HEAD_DIM, (hd + 1) * HEAD_DIM)
            xh = acc[:, cols]
            ms = jnp.mean(xh * xh, axis=-1, keepdims=True)
            xn = xh * lax.rsqrt(ms + RMS_EPS) * gain[:, cols]
            partner = jnp.where(even, pltpu.roll(xn, HEAD_DIM - 1, 1), pltpu.roll(xn, 1, 1))
            o_ref[:, cols] = (xn * cos + partner * sin).astype(BF16)

    @pl.when(n >= n_rope_tiles)
    def _():
        o_ref[...] = (acc * gain).astype(BF16)


def _qkv_projection(h, w_in, col_gain, cos_tab, sin_tab, n_rope_cols):
    s, d = h.shape
    n_cols = w_in.shape[1]
    tm = min(QKV_ROWS, s)
    tn = QKV_COLS
    assert s % tm == 0 and n_cols % tn == 0 and n_rope_cols % tn == 0
    return pl.pallas_call(
        functools.partial(_qkv_kernel, n_rope_tiles=n_rope_cols // tn),
        name="qkv_proj",
        grid=(s // tm, n_cols // tn),
        in_specs=[
            pl.BlockSpec((tm, d), lambda i, n: (i, 0)),
            pl.BlockSpec((d, tn), lambda i, n: (0, n)),
            pl.BlockSpec((1, tn), lambda i, n: (0, n)),
            pl.BlockSpec((tm, HEAD_DIM), lambda i, n: (i, 0)),
            pl.BlockSpec((tm, HEAD_DIM), lambda i, n: (i, 0)),
        ],
        out_specs=pl.BlockSpec((tm, tn), lambda i, n: (i, n)),
        out_shape=jax.ShapeDtypeStruct((s, n_cols), BF16),
        scratch_shapes=[pltpu.VMEM((tm, d), BF16)],
        compiler_params=_compiler_params(("parallel", "arbitrary")),
    )(h, w_in, col_gain, cos_tab, sin_tab)


def _transpose_bf16(x):
    return x.astype(F32).T.astype(BF16)


def _store_transposed_blocks(src_ref, dst_ref):
    n_blocks, _, blk = dst_ref.shape

    def body(c, carry):
        rows = pl.ds(pl.multiple_of(c * blk, blk), blk)
        dst_ref[c] = _transpose_bf16(src_ref[rows, :])
        return carry

    lax.fori_loop(0, n_blocks, body, 0)


def _attn_a_kernel(q_ref, k_ref, v_ref, o_ref, vt_ref, qt_ref, s_ref, smax_ref, m_ref, l_ref, acc_ref, *, tk):
    n_kblocks = k_ref.shape[0] // tk

    @pl.when(pl.program_id(1) == 0)
    def _():
        _store_transposed_blocks(v_ref, vt_ref)

    for r in range(GQA_REP):
        qt_ref[r] = _transpose_bf16(q_ref[:, r * HEAD_DIM:(r + 1) * HEAD_DIM])
    m_ref[...] = jnp.full_like(m_ref, -jnp.inf)
    l_ref[...] = jnp.zeros_like(l_ref)
    acc_ref[...] = jnp.zeros_like(acc_ref)

    def scores(kb, slot, r):
        k = k_ref[pl.ds(pl.multiple_of(kb * tk, tk), tk), :]
        st = jnp.dot(k, qt_ref[r], preferred_element_type=F32)
        s_ref[slot, r] = st
        smax_ref[slot, r] = jnp.max(st, axis=0, keepdims=True)

    def fold(kb, slot, r):
        m_prev = m_ref[r]
        m_new = jnp.maximum(m_prev, smax_ref[slot, r])
        p = jnp.exp2(s_ref[slot, r] - m_new)
        rescale = jnp.exp2(m_prev - m_new)
        l_ref[r] = rescale * l_ref[r] + jnp.sum(p, axis=0, keepdims=True)
        acc_ref[r] = rescale * acc_ref[r] + jnp.dot(vt_ref[kb], p.astype(BF16), preferred_element_type=F32)
        m_ref[r] = m_new

    for r in range(GQA_REP):
        scores(0, 0, r)

    def body(t, carry):
        for slot in range(2):
            kb = 2 * t + slot
            nxt = jnp.minimum(kb + 1, n_kblocks - 1)
            for r in range(GQA_REP):
                scores(nxt, 1 - slot, r)
                fold(kb, slot, r)
        return carry

    lax.fori_loop(0, n_kblocks // 2, body, 0)
    for r in range(GQA_REP):
        o_ref[:, r * HEAD_DIM:(r + 1) * HEAD_DIM] = (acc_ref[r] / l_ref[r]).T.astype(BF16)


def _attention_a(qkv, n_groups, k_col0, v_col0):
    s = qkv.shape[0]
    tq = min(ATT_A_Q, s)
    tk = min(ATT_A_K, s)
    assert s % tq == 0 and s % (2 * tk) == 0
    group_w = GQA_REP * HEAD_DIM
    k_blk0 = k_col0 // HEAD_DIM
    v_blk0 = v_col0 // HEAD_DIM
    return pl.pallas_call(
        functools.partial(_attn_a_kernel, tk=tk),
        name="attn_a",
        grid=(n_groups, s // tq),
        in_specs=[
            pl.BlockSpec((tq, group_w), lambda g, i: (i, g)),
            pl.BlockSpec((s, HEAD_DIM), lambda g, i: (0, k_blk0 + g)),
            pl.BlockSpec((s, HEAD_DIM), lambda g, i: (0, v_blk0 + g)),
        ],
        out_specs=pl.BlockSpec((tq, group_w), lambda g, i: (i, g)),
        out_shape=jax.ShapeDtypeStruct((s, n_groups * group_w), BF16),
        scratch_shapes=[pltpu.VMEM((s // tk, HEAD_DIM, tk), BF16),
                        pltpu.VMEM((GQA_REP, HEAD_DIM, tq), BF16),
                        pltpu.VMEM((2, GQA_REP, tk, tq), F32),
                        pltpu.VMEM((2, GQA_REP, 1, tq), F32),
                        pltpu.VMEM((GQA_REP, 1, tq), F32),
                        pltpu.VMEM((GQA_REP, 1, tq), F32),
                        pltpu.VMEM((GQA_REP, HEAD_DIM, tq), F32)],
        compiler_params=_compiler_params(("parallel", "arbitrary")),
    )(qkv, qkv, qkv)


def _attn_b_kernel(q_ref, k_ref, v_ref, rev_ref, o_ref, vt_ref, t_ref, s_ref, smax_ref, *, blk, n_side):
    n_blocks = k_ref.shape[0] // blk
    n_tiles = 2 * n_side + 1
    _store_transposed_blocks(v_ref, vt_ref)

    for jj in range(n_tiles):
        start = (n_tiles - 1 - jj) * blk
        window = jnp.broadcast_to(rev_ref[0, :, start:start + 2 * blk], (blk, 2 * blk))
        t_ref[jj] = pltpu.roll(window, 0, 1, stride=1, stride_axis=0)[:, blk:]
    t_ref[n_tiles] = jnp.full((blk, blk), MASKED_LOGIT, F32)

    def band(i, jj):
        kb = i + (jj - n_side)
        valid = jnp.logical_and(kb >= 0, kb < n_blocks)
        return jnp.clip(kb, 0, n_blocks - 1), jnp.where(valid, jj, n_tiles)

    def q_transposed(i):
        return _transpose_bf16(q_ref[pl.ds(pl.multiple_of(i * blk, blk), blk), :])

    def scores_tile(i, slot, jj, qt, col_max):
        kb, tile = band(i, jj)
        k = k_ref[pl.ds(pl.multiple_of(kb * blk, blk), blk), :]
        s = jnp.dot(k, qt, preferred_element_type=F32) + t_ref[tile]
        s_ref[slot, jj] = s
        tile_max = jnp.max(s, axis=0, keepdims=True)
        return tile_max if col_max is None else jnp.maximum(col_max, tile_max)

    def fold_tile(i, slot, jj, m, acc, denom):
        kb, _ = band(i, jj)
        p = jnp.exp2(s_ref[slot, jj] - m)
        denom = denom + jnp.sum(p, axis=0, keepdims=True)
        acc = acc + jnp.dot(vt_ref[kb], p.astype(BF16), preferred_element_type=F32)
        return acc, denom

    qt = q_transposed(0)
    col_max = None
    for jj in range(n_tiles):
        col_max = scores_tile(0, 0, jj, qt, col_max)
    smax_ref[0] = col_max

    def body(t, carry):
        for slot in range(2):
            i = 2 * t + slot
            nxt = jnp.minimum(i + 1, n_blocks - 1)
            qt = q_transposed(nxt)
            m = smax_ref[slot]
            acc = jnp.zeros((HEAD_DIM, blk), F32)
            denom = jnp.zeros((1, blk), F32)
            col_max = None
            for jj in range(n_tiles):
                col_max = scores_tile(nxt, 1 - slot, jj, qt, col_max)
                acc, denom = fold_tile(i, slot, jj, m, acc, denom)
            smax_ref[1 - slot] = col_max
            o_ref[pl.ds(pl.multiple_of(i * blk, blk), blk), :] = (acc / denom).T.astype(BF16)
        return carry

    lax.fori_loop(0, n_blocks // 2, body, 0)


def _attention_b(qkv, offset_terms, n_heads, q_col0, k_col0, v_col0):
    s = qkv.shape[0]
    blk = ATT_B_BLK
    n_side = ATT_B_HALO // blk
    n_tiles = 2 * n_side + 1
    assert s % (2 * blk) == 0 and ATT_B_HALO % blk == 0
    q_blk0, k_blk0, v_blk0 = (c // HEAD_DIM for c in (q_col0, k_col0, v_col0))
    return pl.pallas_call(
        functools.partial(_attn_b_kernel, blk=blk, n_side=n_side),
        name="attn_b",
        grid=(n_heads,),
        in_specs=[
            pl.BlockSpec((s, HEAD_DIM), lambda h: (0, q_blk0 + h)),
            pl.BlockSpec((s, HEAD_DIM), lambda h: (0, k_blk0 + h)),
            pl.BlockSpec((s, HEAD_DIM), lambda h: (0, v_blk0 + h)),
            pl.BlockSpec((1, 1, (n_tiles + 1) * blk), lambda h: (h, 0, 0)),
        ],
        out_specs=pl.BlockSpec((s, HEAD_DIM), lambda h: (0, h)),
        out_shape=jax.ShapeDtypeStruct((s, n_heads * HEAD_DIM), BF16),
        scratch_shapes=[pltpu.VMEM((s // blk, HEAD_DIM, blk), BF16),
                        pltpu.VMEM((n_tiles + 1, blk, blk), F32),
                        pltpu.VMEM((2, n_tiles, blk, blk), F32),
                        pltpu.VMEM((2, 1, blk), F32)],
        compiler_params=_compiler_params(("parallel",)),
    )(qkv, qkv, qkv, offset_terms)


def _out_proj_ln_kernel(a_ref, b_ref, w_ref, h_ref, g_ref, beta_ref, o_ref, *, alpha, n_a_chunks):
    k = pl.program_id(1)

    @pl.when(k == 0)
    def _():
        o_ref[...] = alpha * h_ref[...]

    @pl.when(k < n_a_chunks)
    def _():
        o_ref[...] += jnp.dot(a_ref[...], w_ref[...], preferred_element_type=F32)

    @pl.when(k >= n_a_chunks)
    def _():
        o_ref[...] += jnp.dot(b_ref[...], w_ref[...], preferred_element_type=F32)

    @pl.when(k == pl.num_programs(1) - 1)
    def _():
        _layer_norm_in_place(o_ref, g_ref, beta_ref)


def _out_proj_ln(out_a, out_b, w_out, h, ln_g, ln_b, alpha):
    s, d = h.shape
    wa, wb = out_a.shape[1], out_b.shape[1]
    tm = min(OUT_ROWS, s)
    tk = OUT_K
    assert s % tm == 0 and wa % tk == 0 and wb % tk == 0 and tm % LN_CHUNK == 0
    na, nb = wa // tk, wb // tk
    return pl.pallas_call(
        functools.partial(_out_proj_ln_kernel, alpha=alpha, n_a_chunks=na),
        name="out_proj_ln",
        grid=(s // tm, na + nb),
        in_specs=[
            pl.BlockSpec((tm, tk), lambda i, k: (i, jnp.minimum(k, na - 1))),
            pl.BlockSpec((tm, tk), lambda i, k: (i, jnp.maximum(k - na, 0))),
            pl.BlockSpec((tk, d), lambda i, k: (k, 0)),
            pl.BlockSpec((tm, d), lambda i, k: (i, 0)),
            pl.BlockSpec((1, d), lambda i, k: (0, 0)),
            pl.BlockSpec((1, d), lambda i, k: (0, 0)),
        ],
        out_specs=pl.BlockSpec((tm, d), lambda i, k: (i, 0)),
        out_shape=jax.ShapeDtypeStruct((s, d), F32),
        compiler_params=_compiler_params(("parallel", "arbitrary")),
    )(out_a, out_b, w_out, h, ln_g, ln_b)


def _rope_tables(seq_len):
    rows = seq_len // GRID_W
    row = jnp.repeat(jnp.arange(rows, dtype=F32), GRID_W)
    col = jnp.tile(jnp.arange(GRID_W, dtype=F32), rows)
    half = HEAD_DIM // 2
    inv_freq = ROPE_THETA ** (-jnp.arange(0, half, 2, dtype=F32) / half)
    ang = jnp.concatenate([row[:, None] * inv_freq, col[:, None] * inv_freq], axis=-1)
    cos = jnp.repeat(jnp.cos(ang), 2, axis=-1)
    sign = jnp.tile(jnp.array([-1.0, 1.0], F32), HEAD_DIM // 2)
    sin = jnp.repeat(jnp.sin(ang), 2, axis=-1) * sign
    return cos, sin


def _t5_bucket(rel):
    nb = N_REL_BUCKETS // 2
    max_exact = nb // 2
    ret = jnp.where(rel > 0, nb, 0)
    n = jnp.abs(rel)
    large = max_exact + (jnp.log(jnp.maximum(n, 1).astype(F32) / max_exact)
                         / math.log(REL_MAX_DIST / max_exact) * (nb - max_exact)).astype(jnp.int32)
    large = jnp.minimum(large, nb - 1)
    return ret + jnp.where(n < max_exact, n, large)


def _dilated_offset_terms(rel_bias, blk):
    reach = ATT_B_HALO + blk
    off = np.arange(-reach + 1, reach)
    mult = np.zeros(off.shape, np.float64)
    for win, dil in DILATED_CONFIGS:
        mult += (np.abs(off) <= win // 2) & (off % dil == 0)
    log_mult = jnp.asarray(np.log(np.maximum(mult, 1.0)), F32)
    present = jnp.asarray(mult > 0)
    bias = rel_bias[_t5_bucket(jnp.asarray(off, jnp.int32))].astype(F32)
    term = jnp.where(present[:, None], (bias + log_mult[:, None]) * LOG2_E, MASKED_LOGIT).T
    n_heads = term.shape[0]
    return jnp.concatenate([jnp.zeros((n_heads, 1), F32), term[:, ::-1]], axis=1)[:, None, :]


def kernel(x, ffn1_w_gate, ffn1_w_up, ffn1_w_down, ln1_g, ln1_b, w_in, q_norm_g, k_norm_g, rel_bias, w_out,
           ln2_g, ln2_b, ffn2_w_gate, ffn2_w_up, ffn2_w_down, ln3_g, ln3_b):
    batch, s, d = x.shape
    depth = ffn1_w_gate.shape[0]
    assert batch == 1
    n_heads = d // HEAD_DIM
    heads_a = n_heads // 2
    kv_a = heads_a // GQA_REP
    heads_b = n_heads - heads_a
    q_a, kv_a_w, qkv_b = heads_a * HEAD_DIM, kv_a * HEAD_DIM, heads_b * HEAD_DIM
    col_ka, col_va = q_a, q_a + kv_a_w
    col_qb = q_a + 2 * kv_a_w
    col_kb, col_vb = col_qb + qkv_b, col_qb + 2 * qkv_b
    alpha = (2.0 * depth) ** 0.25
    scale = HEAD_DIM ** -0.5 * LOG2_E

    cos_tab, sin_tab = _rope_tables(s)
    offset_terms = _dilated_offset_terms(rel_bias, ATT_B_BLK)

    h = x[0]
    for l in range(depth):
        h = _ffn_ln(h, ffn1_w_gate, ffn1_w_up, ffn1_w_down, l, ln1_g[l][None], ln1_b[l][None], alpha)
        col_gain = jnp.concatenate([
            jnp.tile(q_norm_g[l], heads_a) * scale, jnp.tile(k_norm_g[l], kv_a), jnp.ones((kv_a_w,), F32),
            jnp.full((qkv_b,), scale, F32), jnp.ones((2 * qkv_b,), F32)])[None]
        qkv = _qkv_projection(h, w_in[l].astype(BF16), col_gain, cos_tab, sin_tab, col_va)
        out_a = _attention_a(qkv, kv_a, col_ka, col_va)
        out_b = _attention_b(qkv, offset_terms, heads_b, col_qb, col_kb, col_vb)
        h = _out_proj_ln(out_a, out_b, w_out[l].astype(BF16), h, ln2_g[l][None], ln2_b[l][None], alpha)
        h = _ffn_ln(h, ffn2_w_gate, ffn2_w_up, ffn2_w_down, l, ln3_g[l][None], ln3_b[l][None], alpha)
    return h[None]
```

```python
import functools
import math

import jax
import jax.numpy as jnp
import numpy as np
from jax import lax
from jax.experimental import pallas as pl
from jax.experimental.pallas import tpu as pltpu

F32 = jnp.float32
BF16 = jnp.bfloat16

HEAD_DIM = 128
GQA_REP = 4
GRID_W = 64
ROPE_THETA = 10000.0
DILATED_CONFIGS = ((128, 1), (512, 4), (2048, 16))
N_REL_BUCKETS = 32
REL_MAX_DIST = 1024
LN_EPS = 1e-5
RMS_EPS = 1e-6
MASKED_LOGIT = -1e30
LOG2_E = math.log2(math.e)

V7X_LANES = 128
V7X_SCOPED_VMEM_LIMIT_BYTES = 60000 * 1024

FFN_ROWS = 1024
FFN_COLS = 256
QKV_ROWS = 1024
QKV_COLS = 512
OUT_ROWS = 512
OUT_K = 1024
LN_CHUNK = 128
ATT_A_Q = 512
ATT_A_K = 512
ATT_B_BLK = 256
ATT_B_HALO = max(w for w, _ in DILATED_CONFIGS) // 2


def _compiler_params(semantics):
    return pltpu.CompilerParams(dimension_semantics=semantics,
                                vmem_limit_bytes=V7X_SCOPED_VMEM_LIMIT_BYTES)


def _layer_norm_rows(acc_ref, rows, gain, bias):
    y = acc_ref[rows, :]
    mu = jnp.mean(y, axis=-1, keepdims=True)
    yc = y - mu
    var = jnp.mean(yc * yc, axis=-1, keepdims=True)
    acc_ref[rows, :] = yc * lax.rsqrt(var + LN_EPS) * gain + bias


def _layer_norm_in_place(acc_ref, g_ref, b_ref):
    gain = g_ref[...]
    bias = b_ref[...]

    def body(c, carry):
        _layer_norm_rows(acc_ref, pl.ds(pl.multiple_of(c * LN_CHUNK, LN_CHUNK), LN_CHUNK), gain, bias)
        return carry

    lax.fori_loop(0, acc_ref.shape[0] // LN_CHUNK, body, 0)


def _ffn_ln_kernel(x_hbm, wg_ref, wu_ref, wd_ref, g_ref, b_ref, o_hbm, acc_ref, xb_ref, in_sems, out_sems, *, alpha):
    i = pl.program_id(0)
    j = pl.program_id(1)
    tm = acc_ref.shape[0]
    n_chunks = tm // LN_CHUNK

    def chunk(c):
        return pl.ds(pl.multiple_of(c * LN_CHUNK, LN_CHUNK), LN_CHUNK)

    def hbm_rows(block, c):
        return pl.ds(pl.multiple_of(block * tm + c * LN_CHUNK, LN_CHUNK), LN_CHUNK)

    def load_x(block, c):
        return pltpu.make_async_copy(x_hbm.at[hbm_rows(block, c), :], acc_ref.at[chunk(c), :], in_sems.at[c])

    def store_out(c):
        return pltpu.make_async_copy(acc_ref.at[chunk(c), :], o_hbm.at[hbm_rows(i, c), :], out_sems.at[c])

    def stage(c):
        x = acc_ref[chunk(c), :]
        xb_ref[chunk(c), :] = x.astype(BF16)
        acc_ref[chunk(c), :] = alpha * x

    def for_chunks(n, fn):
        def body(c, carry):
            fn(c)
            return carry
        lax.fori_loop(0, n, body, 0)

    @pl.when(jnp.logical_and(i == 0, j == 0))
    def _():
        for_chunks(n_chunks, lambda c: load_x(0, c).start())

        def arrive(c):
            load_x(0, c).wait()
            stage(c)
        for_chunks(n_chunks, arrive)

    xb = xb_ref[...]
    gate = jnp.dot(xb, wg_ref[...].astype(BF16), preferred_element_type=F32)
    up = jnp.dot(xb, wu_ref[...].astype(BF16), preferred_element_type=F32)
    hidden = (0.5 * gate * jax.nn.sigmoid(gate) * up).astype(BF16)
    acc_ref[...] += jnp.dot(hidden, wd_ref[...].astype(BF16), preferred_element_type=F32)

    is_last_step = j == pl.num_programs(1) - 1
    has_next_block = i + 1 < pl.num_programs(0)

    def normalise_and_store(c):
        _layer_norm_rows(acc_ref, chunk(c), g_ref[...], b_ref[...])
        store_out(c).start()

    @pl.when(jnp.logical_and(is_last_step, has_next_block))
    def _():
        def pipeline(c):
            @pl.when(c < n_chunks)
            def _():
                normalise_and_store(c)

            @pl.when(jnp.logical_and(c >= 1, c <= n_chunks))
            def _():
                store_out(c - 1).wait()
                load_x(i + 1, c - 1).start()

            @pl.when(c >= 2)
            def _():
                load_x(i + 1, c - 2).wait()
                stage(c - 2)
        for_chunks(n_chunks + 2, pipeline)

    @pl.when(jnp.logical_and(is_last_step, jnp.logical_not(has_next_block)))
    def _():
        for_chunks(n_chunks, normalise_and_store)
        for_chunks(n_chunks, lambda c: store_out(c).wait())


def _ffn_ln(x, wg, wu, wd, layer, ln_g, ln_b, alpha):
    s, d = x.shape
    f = wg.shape[2]
    tm = min(FFN_ROWS, s)
    tf = min(FFN_COLS, f)
    assert s % tm == 0 and f % tf == 0 and tm % LN_CHUNK == 0
    return pl.pallas_call(
        functools.partial(_ffn_ln_kernel, alpha=alpha),
        name="ffn_ln",
        grid=(s // tm, f // tf),
        in_specs=[
            pl.BlockSpec(memory_space=pl.ANY),
            pl.BlockSpec((None, d, tf), lambda i, j: (layer, 0, j)),
            pl.BlockSpec((None, d, tf), lambda i, j: (layer, 0, j)),
            pl.BlockSpec((None, tf, d), lambda i, j: (layer, j, 0)),
            pl.BlockSpec((1, d), lambda i, j: (0, 0)),
            pl.BlockSpec((1, d), lambda i, j: (0, 0)),
        ],
        out_specs=pl.BlockSpec(memory_space=pl.ANY),
        out_shape=jax.ShapeDtypeStruct((s, d), F32),
        scratch_shapes=[pltpu.VMEM((tm, d), F32), pltpu.VMEM((tm, d), BF16),
                        pltpu.SemaphoreType.DMA((tm // LN_CHUNK,)), pltpu.SemaphoreType.DMA((tm // LN_CHUNK,))],
        compiler_params=_compiler_params(("arbitrary", "arbitrary")),
    )(x, wg, wu, wd, ln_g, ln_b)


def _qkv_kernel(h_hbm, w_ref, gain_ref, cos_ref, sin_ref, o_ref, hs_ref, hb_ref, h_sem, *, n_rope_tiles):
    i = pl.program_id(0)
    n = pl.program_id(1)
    tm = hs_ref.shape[0]

    def load_h(block):
        rows = pl.ds(pl.multiple_of(block * tm, tm), tm)
        return pltpu.make_async_copy(h_hbm.at[rows, :], hs_ref, h_sem)

    @pl.when(jnp.logical_and(i == 0, n == 0))
    def _():
        load_h(0).start()

    @pl.when(n == 0)
    def _():
        load_h(i).wait()
        hb_ref[...] = hs_ref[...].astype(BF16)

        @pl.when(i + 1 < pl.num_programs(0))
        def _():
            load_h(i + 1).start()

    acc = jnp.dot(hb_ref[...], w_ref[...].astype(BF16), preferred_element_type=F32)
    gain = gain_ref[...]

    @pl.when(n < n_rope_tiles)
    def _():
        cos = cos_ref[...]
        sin = sin_ref[...]
        lane = lax.broadcasted_iota(jnp.int32, cos.shape, 1)
        even = (lane & 1) == 0
        for hd in range(acc.shape[1] // HEAD_DIM):
            cols = slice(hd * HEAD_DIM, (hd + 1) * HEAD_DIM)
            xh = acc[:, cols]
            ms = jnp.mean(xh * xh, axis=-1, keepdims=True)
            xn = xh * lax.rsqrt(ms + RMS_EPS) * gain[:, cols]
            partner = jnp.where(even, pltpu.roll(xn, HEAD_DIM - 1, 1), pltpu.roll(xn, 1, 1))
            o_ref[:, cols] = (xn * cos + partner * sin).astype(BF16)

    @pl.when(n >= n_rope_tiles)
    def _():
        o_ref[...] = (acc * gain).astype(BF16)


def _qkv_projection(h, w_in, layer, col_gain, cos_tab, sin_tab, n_rope_cols):
    s, d = h.shape
    n_cols = w_in.shape[2]
    tm = min(QKV_ROWS, s)
    tn = QKV_COLS
    assert s % tm == 0 and n_cols % tn == 0 and n_rope_cols % tn == 0
    return pl.pallas_call(
        functools.partial(_qkv_kernel, n_rope_tiles=n_rope_cols // tn),
        name="qkv_proj",
        grid=(s // tm, n_cols // tn),
        in_specs=[
            pl.BlockSpec(memory_space=pl.ANY),
            pl.BlockSpec((None, d, tn), lambda i, n: (layer, 0, n)),
            pl.BlockSpec((1, tn), lambda i, n: (0, n)),
            pl.BlockSpec((tm, HEAD_DIM), lambda i, n: (i, 0)),
            pl.BlockSpec((tm, HEAD_DIM), lambda i, n: (i, 0)),
        ],
        out_specs=pl.BlockSpec((tm, tn), lambda i, n: (i, n)),
        out_shape=jax.ShapeDtypeStruct((s, n_cols), BF16),
        scratch_shapes=[pltpu.VMEM((tm, d), F32), pltpu.VMEM((tm, d), BF16), pltpu.SemaphoreType.DMA(())],
        compiler_params=_compiler_params(("arbitrary", "arbitrary")),
    )(h, w_in, col_gain, cos_tab, sin_tab)


def _transpose_bf16(x):
    return x.astype(F32).T.astype(BF16)


def _store_transposed_blocks(src_ref, dst_ref):
    n_blocks, _, blk = dst_ref.shape

    def body(c, carry):
        rows = pl.ds(pl.multiple_of(c * blk, blk), blk)
        dst_ref[c] = _transpose_bf16(src_ref[rows, :])
        return carry

    lax.fori_loop(0, n_blocks, body, 0)


def _attn_a_kernel(q_ref, k_ref, v_ref, o_ref, vt_ref, qt_ref, s_ref, smax_ref, m_ref, l_ref, acc_ref, *, tk):
    n_kblocks = k_ref.shape[0] // tk

    @pl.when(pl.program_id(1) == 0)
    def _():
        _store_transposed_blocks(v_ref, vt_ref)

    for r in range(GQA_REP):
        qt_ref[r] = _transpose_bf16(q_ref[:, r * HEAD_DIM:(r + 1) * HEAD_DIM])
    m_ref[...] = jnp.full_like(m_ref, -jnp.inf)
    l_ref[...] = jnp.zeros_like(l_ref)
    acc_ref[...] = jnp.zeros_like(acc_ref)

    def scores(kb, slot, r):
        k = k_ref[pl.ds(pl.multiple_of(kb * tk, tk), tk), :]
        st = jnp.dot(k, qt_ref[r], preferred_element_type=F32)
        s_ref[slot, r] = st
        smax_ref[slot, r] = jnp.max(st, axis=0, keepdims=True)

    def fold(kb, slot, r):
        m_prev = m_ref[r]
        m_new = jnp.maximum(m_prev, smax_ref[slot, r])
        p = jnp.exp2(s_ref[slot, r] - m_new)
        rescale = jnp.exp2(m_prev - m_new)
        l_ref[r] = rescale * l_ref[r] + jnp.sum(p, axis=0, keepdims=True)
        acc_ref[r] = rescale * acc_ref[r] + jnp.dot(vt_ref[kb], p.astype(BF16), preferred_element_type=F32)
        m_ref[r] = m_new

    for r in range(GQA_REP):
        scores(0, 0, r)

    def body(t, carry):
        for slot in range(2):
            kb = 2 * t + slot
            nxt = jnp.minimum(kb + 1, n_kblocks - 1)
            for r in range(GQA_REP):
                scores(nxt, 1 - slot, r)
                fold(kb, slot, r)
        return carry

    lax.fori_loop(0, n_kblocks // 2, body, 0)
    for r in range(GQA_REP):
        o_ref[:, r * HEAD_DIM:(r + 1) * HEAD_DIM] = (acc_ref[r] / l_ref[r]).T.astype(BF16)


def _attention_a(qkv, n_groups, k_col0, v_col0):
    s = qkv.shape[0]
    tq = min(ATT_A_Q, s)
    tk = min(ATT_A_K, s)
    assert s % tq == 0 and s % (2 * tk) == 0
    group_w = GQA_REP * HEAD_DIM
    k_blk0 = k_col0 // HEAD_DIM
    v_blk0 = v_col0 // HEAD_DIM
    return pl.pallas_call(
        functools.partial(_attn_a_kernel, tk=tk),
        name="attn_a",
        grid=(n_groups, s // tq),
        in_specs=[
            pl.BlockSpec((tq, group_w), lambda g, i: (i, g)),
            pl.BlockSpec((s, HEAD_DIM), lambda g, i: (0, k_blk0 + g)),
            pl.BlockSpec((s, HEAD_DIM), lambda g, i: (0, v_blk0 + g)),
        ],
        out_specs=pl.BlockSpec((tq, group_w), lambda g, i: (i, g)),
        out_shape=jax.ShapeDtypeStruct((s, n_groups * group_w), BF16),
        scratch_shapes=[pltpu.VMEM((s // tk, HEAD_DIM, tk), BF16),
                        pltpu.VMEM((GQA_REP, HEAD_DIM, tq), BF16),
                        pltpu.VMEM((2, GQA_REP, tk, tq), F32),
                        pltpu.VMEM((2, GQA_REP, 1, tq), F32),
                        pltpu.VMEM((GQA_REP, 1, tq), F32),
                        pltpu.VMEM((GQA_REP, 1, tq), F32),
                        pltpu.VMEM((GQA_REP, HEAD_DIM, tq), F32)],
        compiler_params=_compiler_params(("parallel", "arbitrary")),
    )(qkv, qkv, qkv)


def _attn_b_kernel(q_ref, k_ref, v_ref, rev_ref, o_ref, vt_ref, t_ref, s_ref, smax_ref, *, blk, n_side):
    n_blocks = k_ref.shape[0] // blk
    n_tiles = 2 * n_side + 1
    _store_transposed_blocks(v_ref, vt_ref)

    for jj in range(n_tiles):
        start = (n_tiles - 1 - jj) * blk
        window = jnp.broadcast_to(rev_ref[0, :, start:start + 2 * blk], (blk, 2 * blk))
        t_ref[jj] = pltpu.roll(window, 0, 1, stride=1, stride_axis=0)[:, blk:]
    t_ref[n_tiles] = jnp.full((blk, blk), MASKED_LOGIT, F32)

    def band(i, jj):
        kb = i + (jj - n_side)
        valid = jnp.logical_and(kb >= 0, kb < n_blocks)
        return jnp.clip(kb, 0, n_blocks - 1), jnp.where(valid, jj, n_tiles)

    def q_transposed(i):
        return _transpose_bf16(q_ref[pl.ds(pl.multiple_of(i * blk, blk), blk), :])

    def scores_tile(i, slot, jj, qt, col_max):
        kb, tile = band(i, jj)
        k = k_ref[pl.ds(pl.multiple_of(kb * blk, blk), blk), :]
        s = jnp.dot(k, qt, preferred_element_type=F32) + t_ref[tile]
        s_ref[slot, jj] = s
        tile_max = jnp.max(s, axis=0, keepdims=True)
        return tile_max if col_max is None else jnp.maximum(col_max, tile_max)

    def fold_tile(i, slot, jj, m, acc, denom):
        kb, _ = band(i, jj)
        p = jnp.exp2(s_ref[slot, jj] - m)
        denom = denom + jnp.sum(p, axis=0, keepdims=True)
        acc = acc + jnp.dot(vt_ref[kb], p.astype(BF16), preferred_element_type=F32)
        return acc, denom

    qt = q_transposed(0)
    col_max = None
    for jj in range(n_tiles):
        col_max = scores_tile(0, 0, jj, qt, col_max)
    smax_ref[0] = col_max

    def body(t, carry):
        for slot in range(2):
            i = 2 * t + slot
            nxt = jnp.minimum(i + 1, n_blocks - 1)
            qt = q_transposed(nxt)
            m = smax_ref[slot]
            acc = jnp.zeros((HEAD_DIM, blk), F32)
            denom = jnp.zeros((1, blk), F32)
            col_max = None
            for jj in range(n_tiles):
                col_max = scores_tile(nxt, 1 - slot, jj, qt, col_max)
                acc, denom = fold_tile(i, slot, jj, m, acc, denom)
            smax_ref[1 - slot] = col_max
            o_ref[pl.ds(pl.multiple_of(i * blk, blk), blk), :] = (acc / denom).T.astype(BF16)
        return carry

    lax.fori_loop(0, n_blocks // 2, body, 0)


def _attention_b(qkv, offset_terms, n_heads, q_col0, k_col0, v_col0):
    s = qkv.shape[0]
    blk = ATT_B_BLK
    n_side = ATT_B_HALO // blk
    n_tiles = 2 * n_side + 1
    assert s % (2 * blk) == 0 and ATT_B_HALO % blk == 0
    q_blk0, k_blk0, v_blk0 = (c // HEAD_DIM for c in (q_col0, k_col0, v_col0))
    return pl.pallas_call(
        functools.partial(_attn_b_kernel, blk=blk, n_side=n_side),
        name="attn_b",
        grid=(n_heads,),
        in_specs=[
            pl.BlockSpec((s, HEAD_DIM), lambda h: (0, q_blk0 + h)),
            pl.BlockSpec((s, HEAD_DIM), lambda h: (0, k_blk0 + h)),
            pl.BlockSpec((s, HEAD_DIM), lambda h: (0, v_blk0 + h)),
            pl.BlockSpec((1, 1, (n_tiles + 1) * blk), lambda h: (h, 0, 0)),
        ],
        out_specs=pl.BlockSpec((s, HEAD_DIM), lambda h: (0, h)),
        out_shape=jax.ShapeDtypeStruct((s, n_heads * HEAD_DIM), BF16),
        scratch_shapes=[pltpu.VMEM((s // blk, HEAD_DIM, blk), BF16),
                        pltpu.VMEM((n_tiles + 1, blk, blk), F32),
                        pltpu.VMEM((2, n_tiles, blk, blk), F32),
                        pltpu.VMEM((2, 1, blk), F32)],
        compiler_params=_compiler_params(("parallel",)),
    )(qkv, qkv, qkv, offset_terms)


def _out_proj_ln_kernel(a_ref, b_ref, w_ref, h_ref, g_ref, beta_ref, o_ref, *, alpha, n_a_chunks):
    k = pl.program_id(1)

    @pl.when(k == 0)
    def _():
        o_ref[...] = alpha * h_ref[...]

    @pl.when(k < n_a_chunks)
    def _():
        o_ref[...] += jnp.dot(a_ref[...], w_ref[...], preferred_element_type=F32)

    @pl.when(k >= n_a_chunks)
    def _():
        o_ref[...] += jnp.dot(b_ref[...], w_ref[...], preferred_element_type=F32)

    @pl.when(k == pl.num_programs(1) - 1)
    def _():
        _layer_norm_in_place(o_ref, g_ref, beta_ref)


def _out_proj_ln(out_a, out_b, w_out, h, ln_g, ln_b, alpha):
    s, d = h.shape
    wa, wb = out_a.shape[1], out_b.shape[1]
    tm = min(OUT_ROWS, s)
    tk = OUT_K
    assert s % tm == 0 and wa % tk == 0 and wb % tk == 0 and tm % LN_CHUNK == 0
    na, nb = wa // tk, wb // tk
    return pl.pallas_call(
        functools.partial(_out_proj_ln_kernel, alpha=alpha, n_a_chunks=na),
        name="out_proj_ln",
        grid=(s // tm, na + nb),
        in_specs=[
            pl.BlockSpec((tm, tk), lambda i, k: (i, jnp.minimum(k, na - 1))),
            pl.BlockSpec((tm, tk), lambda i, k: (i, jnp.maximum(k - na, 0))),
            pl.BlockSpec((tk, d), lambda i, k: (k, 0)),
            pl.BlockSpec((tm, d), lambda i, k: (i, 0)),
            pl.BlockSpec((1, d), lambda i, k: (0, 0)),
            pl.BlockSpec((1, d), lambda i, k: (0, 0)),
        ],
        out_specs=pl.BlockSpec((tm, d), lambda i, k: (i, 0)),
        out_shape=jax.ShapeDtypeStruct((s, d), F32),
        compiler_params=_compiler_params(("parallel", "arbitrary")),
    )(out_a, out_b, w_out, h, ln_g, ln_b)


def _rope_tables(seq_len):
    rows = seq_len // GRID_W
    row = jnp.repeat(jnp.arange(rows, dtype=F32), GRID_W)
    col = jnp.tile(jnp.arange(GRID_W, dtype=F32), rows)
    half = HEAD_DIM // 2
    inv_freq = ROPE_THETA ** (-jnp.arange(0, half, 2, dtype=F32) / half)
    ang = jnp.concatenate([row[:, None] * inv_freq, col[:, None] * inv_freq], axis=-1)
    cos = jnp.repeat(jnp.cos(ang), 2, axis=-1)
    sign = jnp.tile(jnp.array([-1.0, 1.0], F32), HEAD_DIM // 2)
    sin = jnp.repeat(jnp.sin(ang), 2, axis=-1) * sign
    return cos, sin


def _t5_bucket(rel):
    nb = N_REL_BUCKETS // 2
    max_exact = nb // 2
    ret = jnp.where(rel > 0, nb, 0)
    n = jnp.abs(rel)
    large = max_exact + (jnp.log(jnp.maximum(n, 1).astype(F32) / max_exact)
                         / math.log(REL_MAX_DIST / max_exact) * (nb - max_exact)).astype(jnp.int32)
    large = jnp.minimum(large, nb - 1)
    return ret + jnp.where(n < max_exact, n, large)


def _dilated_offset_terms(rel_bias, blk):
    reach = ATT_B_HALO + blk
    off = np.arange(-reach + 1, reach)
    mult = np.zeros(off.shape, np.float64)
    for win, dil in DILATED_CONFIGS:
        mult += (np.abs(off) <= win // 2) & (off % dil == 0)
    log_mult = jnp.asarray(np.log(np.maximum(mult, 1.0)), F32)
    present = jnp.asarray(mult > 0)
    bias = rel_bias[_t5_bucket(jnp.asarray(off, jnp.int32))].astype(F32)
    term = jnp.where(present[:, None], (bias + log_mult[:, None]) * LOG2_E, MASKED_LOGIT).T
    n_heads = term.shape[0]
    return jnp.concatenate([jnp.zeros((n_heads, 1), F32), term[:, ::-1]], axis=1)[:, None, :]


def kernel(x, ffn1_w_gate, ffn1_w_up, ffn1_w_down, ln1_g, ln1_b, w_in, q_norm_g, k_norm_g, rel_bias, w_out,
           ln2_g, ln2_b, ffn2_w_gate, ffn2_w_up, ffn2_w_down, ln3_g, ln3_b):
    batch, s, d = x.shape
    depth = ffn1_w_gate.shape[0]
    assert batch == 1
    n_heads = d // HEAD_DIM
    heads_a = n_heads // 2
    kv_a = heads_a // GQA_REP
    heads_b = n_heads - heads_a
    q_a, kv_a_w, qkv_b = heads_a * HEAD_DIM, kv_a * HEAD_DIM, heads_b * HEAD_DIM
    col_ka, col_va = q_a, q_a + kv_a_w
    col_qb = q_a + 2 * kv_a_w
    col_kb, col_vb = col_qb + qkv_b, col_qb + 2 * qkv_b
    alpha = (2.0 * depth) ** 0.25
    scale = HEAD_DIM ** -0.5 * LOG2_E

    cos_tab, sin_tab = _rope_tables(s)
    offset_terms = _dilated_offset_terms(rel_bias, ATT_B_BLK)

    h = x[0]
    for l in range(depth):
        h = _ffn_ln(h, ffn1_w_gate, ffn1_w_up, ffn1_w_down, l, ln1_g[l][None], ln1_b[l][None], alpha)
        col_gain = jnp.concatenate([
            jnp.tile(q_norm_g[l], heads_a) * scale, jnp.tile(k_norm_g[l], kv_a), jnp.ones((kv_a_w,), F32),
            jnp.full((qkv_b,), scale, F32), jnp.ones((2 * qkv_b,), F32)])[None]
        qkv = _qkv_projection(h, w_in, l, col_gain, cos_tab, sin_tab, col_va)
        out_a = _attention_a(qkv, kv_a, col_ka, col_va)
        out_b = _attention_b(qkv, offset_terms, heads_b, col_qb, col_kb, col_vb)
        h = _out_proj_ln(out_a, out_b, w_out[l].astype(BF16), h, ln2_g[l][None], ln2_b[l][None], alpha)
        h = _ffn_ln(h, ffn2_w_gate, ffn2_w_up, ffn2_w_down, l, ln3_g[l][None], ln3_b[l][None], alpha)
    return h[None]
```

```python
import functools
import math

import jax
import jax.numpy as jnp
import numpy as np
from jax import lax
from jax.experimental import pallas as pl
from jax.experimental.pallas import tpu as pltpu

F32 = jnp.float32
BF16 = jnp.bfloat16

HEAD_DIM = 128
GQA_REP = 4
GRID_W = 64
ROPE_THETA = 10000.0
DILATED_CONFIGS = ((128, 1), (512, 4), (2048, 16))
N_REL_BUCKETS = 32
REL_MAX_DIST = 1024
LN_EPS = 1e-5
RMS_EPS = 1e-6
MASKED_LOGIT = -1e30
LOG2_E = math.log2(math.e)

V7X_LANES = 128
V7X_SCOPED_VMEM_LIMIT_BYTES = 60000 * 1024

FFN_ROWS = 1024
FFN_COLS = 256
QKV_ROWS = 1024
QKV_COLS = 512
OUT_ROWS = 512
OUT_K = 1024
LN_CHUNK = 128
ATT_A_Q = 512
ATT_A_K = 512
ATT_B_BLK = 256
ATT_UNROLL = 8
ATT_B_HALO = max(w for w, _ in DILATED_CONFIGS) // 2


def _compiler_params(semantics):
    return pltpu.CompilerParams(dimension_semantics=semantics,
                                vmem_limit_bytes=V7X_SCOPED_VMEM_LIMIT_BYTES)


def _layer_norm_rows(acc_ref, rows, gain, bias):
    y = acc_ref[rows, :]
    mu = jnp.mean(y, axis=-1, keepdims=True)
    yc = y - mu
    var = jnp.mean(yc * yc, axis=-1, keepdims=True)
    acc_ref[rows, :] = yc * lax.rsqrt(var + LN_EPS) * gain + bias


def _layer_norm_in_place(acc_ref, g_ref, b_ref):
    gain = g_ref[...]
    bias = b_ref[...]

    def body(c, carry):
        _layer_norm_rows(acc_ref, pl.ds(pl.multiple_of(c * LN_CHUNK, LN_CHUNK), LN_CHUNK), gain, bias)
        return carry

    lax.fori_loop(0, acc_ref.shape[0] // LN_CHUNK, body, 0)


def _ffn_ln_kernel(x_hbm, wg_ref, wu_ref, wd_ref, g_ref, b_ref, o_hbm, acc_ref, xb_ref, in_sems, out_sems, *, alpha):
    i = pl.program_id(0)
    j = pl.program_id(1)
    tm = acc_ref.shape[0]
    n_chunks = tm // LN_CHUNK

    def chunk(c):
        return pl.ds(pl.multiple_of(c * LN_CHUNK, LN_CHUNK), LN_CHUNK)

    def hbm_rows(block, c):
        return pl.ds(pl.multiple_of(block * tm + c * LN_CHUNK, LN_CHUNK), LN_CHUNK)

    def load_x(block, c):
        return pltpu.make_async_copy(x_hbm.at[hbm_rows(block, c), :], acc_ref.at[chunk(c), :], in_sems.at[c])

    def store_out(c):
        return pltpu.make_async_copy(acc_ref.at[chunk(c), :], o_hbm.at[hbm_rows(i, c), :], out_sems.at[c])

    def stage(c):
        x = acc_ref[chunk(c), :]
        xb_ref[chunk(c), :] = x.astype(BF16)
        acc_ref[chunk(c), :] = alpha * x

    def for_chunks(n, fn):
        def body(c, carry):
            fn(c)
            return carry
        lax.fori_loop(0, n, body, 0)

    @pl.when(jnp.logical_and(i == 0, j == 0))
    def _():
        for_chunks(n_chunks, lambda c: load_x(0, c).start())

        def arrive(c):
            load_x(0, c).wait()
            stage(c)
        for_chunks(n_chunks, arrive)

    xb = xb_ref[...]
    gate = jnp.dot(xb, wg_ref[...].astype(BF16), preferred_element_type=F32)
    up = jnp.dot(xb, wu_ref[...].astype(BF16), preferred_element_type=F32)
    hidden = (0.5 * gate * jax.nn.sigmoid(gate) * up).astype(BF16)
    acc_ref[...] += jnp.dot(hidden, wd_ref[...].astype(BF16), preferred_element_type=F32)

    is_last_step = j == pl.num_programs(1) - 1
    has_next_block = i + 1 < pl.num_programs(0)

    def normalise_and_store(c):
        _layer_norm_rows(acc_ref, chunk(c), g_ref[...], b_ref[...])
        store_out(c).start()

    @pl.when(jnp.logical_and(is_last_step, has_next_block))
    def _():
        def pipeline(c):
            @pl.when(c < n_chunks)
            def _():
                normalise_and_store(c)

            @pl.when(jnp.logical_and(c >= 1, c <= n_chunks))
            def _():
                store_out(c - 1).wait()
                load_x(i + 1, c - 1).start()

            @pl.when(c >= 2)
            def _():
                load_x(i + 1, c - 2).wait()
                stage(c - 2)
        for_chunks(n_chunks + 2, pipeline)

    @pl.when(jnp.logical_and(is_last_step, jnp.logical_not(has_next_block)))
    def _():
        for_chunks(n_chunks, normalise_and_store)
        for_chunks(n_chunks, lambda c: store_out(c).wait())


def _ffn_ln(x, wg, wu, wd, layer, ln_g, ln_b, alpha):
    s, d = x.shape
    f = wg.shape[2]
    tm = min(FFN_ROWS, s)
    tf = min(FFN_COLS, f)
    assert s % tm == 0 and f % tf == 0 and tm % LN_CHUNK == 0
    return pl.pallas_call(
        functools.partial(_ffn_ln_kernel, alpha=alpha),
        name="ffn_ln",
        grid=(s // tm, f // tf),
        in_specs=[
            pl.BlockSpec(memory_space=pl.ANY),
            pl.BlockSpec((None, d, tf), lambda i, j: (layer, 0, j)),
            pl.BlockSpec((None, d, tf), lambda i, j: (layer, 0, j)),
            pl.BlockSpec((None, tf, d), lambda i, j: (layer, j, 0)),
            pl.BlockSpec((1, d), lambda i, j: (0, 0)),
            pl.BlockSpec((1, d), lambda i, j: (0, 0)),
        ],
        out_specs=pl.BlockSpec(memory_space=pl.ANY),
        out_shape=jax.ShapeDtypeStruct((s, d), F32),
        scratch_shapes=[pltpu.VMEM((tm, d), F32), pltpu.VMEM((tm, d), BF16),
                        pltpu.SemaphoreType.DMA((tm // LN_CHUNK,)), pltpu.SemaphoreType.DMA((tm // LN_CHUNK,))],
        compiler_params=_compiler_params(("arbitrary", "arbitrary")),
    )(x, wg, wu, wd, ln_g, ln_b)


def _qkv_kernel(h_hbm, w_ref, gain_ref, cos_ref, sin_ref, o_ref, hs_ref, hb_ref, h_sem, *, n_rope_tiles):
    i = pl.program_id(0)
    n = pl.program_id(1)
    tm = hs_ref.shape[0]

    def load_h(block):
        rows = pl.ds(pl.multiple_of(block * tm, tm), tm)
        return pltpu.make_async_copy(h_hbm.at[rows, :], hs_ref, h_sem)

    @pl.when(jnp.logical_and(i == 0, n == 0))
    def _():
        load_h(0).start()

    @pl.when(n == 0)
    def _():
        load_h(i).wait()
        hb_ref[...] = hs_ref[...].astype(BF16)

        @pl.when(i + 1 < pl.num_programs(0))
        def _():
            load_h(i + 1).start()

    acc = jnp.dot(hb_ref[...], w_ref[...].astype(BF16), preferred_element_type=F32)
    gain = gain_ref[...]

    @pl.when(n < n_rope_tiles)
    def _():
        cos = cos_ref[...]
        sin = sin_ref[...]
        lane = lax.broadcasted_iota(jnp.int32, cos.shape, 1)
        even = (lane & 1) == 0
        for hd in range(acc.shape[1] // HEAD_DIM):
            cols = slice(hd * HEAD_DIM, (hd + 1) * HEAD_DIM)
            xh = acc[:, cols]
            ms = jnp.mean(xh * xh, axis=-1, keepdims=True)
            xn = xh * lax.rsqrt(ms + RMS_EPS) * gain[:, cols]
            partner = jnp.where(even, pltpu.roll(xn, HEAD_DIM - 1, 1), pltpu.roll(xn, 1, 1))
            o_ref[:, cols] = (xn * cos + partner * sin).astype(BF16)

    @pl.when(n >= n_rope_tiles)
    def _():
        o_ref[...] = (acc * gain).astype(BF16)


def _qkv_projection(h, w_in, layer, col_gain, cos_tab, sin_tab, n_rope_cols):
    s, d = h.shape
    n_cols = w_in.shape[2]
    tm = min(QKV_ROWS, s)
    tn = QKV_COLS
    assert s % tm == 0 and n_cols % tn == 0 and n_rope_cols % tn == 0
    return pl.pallas_call(
        functools.partial(_qkv_kernel, n_rope_tiles=n_rope_cols // tn),
        name="qkv_proj",
        grid=(s // tm, n_cols // tn),
        in_specs=[
            pl.BlockSpec(memory_space=pl.ANY),
            pl.BlockSpec((None, d, tn), lambda i, n: (layer, 0, n)),
            pl.BlockSpec((1, tn), lambda i, n: (0, n)),
            pl.BlockSpec((tm, HEAD_DIM), lambda i, n: (i, 0)),
            pl.BlockSpec((tm, HEAD_DIM), lambda i, n: (i, 0)),
        ],
        out_specs=pl.BlockSpec((tm, tn), lambda i, n: (i, n)),
        out_shape=jax.ShapeDtypeStruct((s, n_cols), BF16),
        scratch_shapes=[pltpu.VMEM((tm, d), F32), pltpu.VMEM((tm, d), BF16), pltpu.SemaphoreType.DMA(())],
        compiler_params=_compiler_params(("arbitrary", "arbitrary")),
    )(h, w_in, col_gain, cos_tab, sin_tab)


def _transpose_bf16(x):
    return x.astype(F32).T.astype(BF16)


def _store_transposed_blocks(src_ref, dst_ref):
    n_blocks, _, blk = dst_ref.shape

    def body(c, carry):
        rows = pl.ds(pl.multiple_of(c * blk, blk), blk)
        dst_ref[c] = _transpose_bf16(src_ref[rows, :])
        return carry

    lax.fori_loop(0, n_blocks, body, 0, unroll=min(ATT_UNROLL, n_blocks))


def _attn_a_kernel(q_ref, k_ref, v_ref, o_ref, vt_ref, qt_ref, s_ref, smax_ref, m_ref, l_ref, acc_ref, *, tk):
    n_kblocks = k_ref.shape[0] // tk

    @pl.when(pl.program_id(1) == 0)
    def _():
        _store_transposed_blocks(v_ref, vt_ref)

    for r in range(GQA_REP):
        qt_ref[r] = _transpose_bf16(q_ref[:, r * HEAD_DIM:(r + 1) * HEAD_DIM])
    m_ref[...] = jnp.full_like(m_ref, -jnp.inf)
    l_ref[...] = jnp.zeros_like(l_ref)
    acc_ref[...] = jnp.zeros_like(acc_ref)

    def scores(kb, slot, r):
        k = k_ref[pl.ds(pl.multiple_of(kb * tk, tk), tk), :]
        st = jnp.dot(k, qt_ref[r], preferred_element_type=F32)
        s_ref[slot, r] = st
        smax_ref[slot, r] = jnp.max(st, axis=0, keepdims=True)

    def fold(kb, slot, r):
        m_prev = m_ref[r]
        m_new = jnp.maximum(m_prev, smax_ref[slot, r])
        p = jnp.exp2(s_ref[slot, r] - m_new)
        rescale = jnp.exp2(m_prev - m_new)
        l_ref[r] = rescale * l_ref[r] + jnp.sum(p, axis=0, keepdims=True)
        acc_ref[r] = rescale * acc_ref[r] + jnp.dot(vt_ref[kb], p.astype(BF16), preferred_element_type=F32)
        m_ref[r] = m_new

    for r in range(GQA_REP):
        scores(0, 0, r)

    unroll = min(ATT_UNROLL, n_kblocks)

    def body(t, carry):
        for u in range(unroll):
            slot = u % 2
            kb = unroll * t + u
            nxt = jnp.minimum(kb + 1, n_kblocks - 1)
            for r in range(GQA_REP):
                scores(nxt, 1 - slot, r)
                fold(kb, slot, r)
        return carry

    lax.fori_loop(0, n_kblocks // unroll, body, 0)
    for r in range(GQA_REP):
        o_ref[:, r * HEAD_DIM:(r + 1) * HEAD_DIM] = (acc_ref[r] / l_ref[r]).T.astype(BF16)


def _attention_a(qkv, n_groups, k_col0, v_col0):
    s = qkv.shape[0]
    tq = min(ATT_A_Q, s)
    tk = min(ATT_A_K, s)
    assert s % tq == 0 and (s // tk) % min(ATT_UNROLL, s // tk) == 0
    group_w = GQA_REP * HEAD_DIM
    k_blk0 = k_col0 // HEAD_DIM
    v_blk0 = v_col0 // HEAD_DIM
    return pl.pallas_call(
        functools.partial(_attn_a_kernel, tk=tk),
        name="attn_a",
        grid=(n_groups, s // tq),
        in_specs=[
            pl.BlockSpec((tq, group_w), lambda g, i: (i, g)),
            pl.BlockSpec((s, HEAD_DIM), lambda g, i: (0, k_blk0 + g)),
            pl.BlockSpec((s, HEAD_DIM), lambda g, i: (0, v_blk0 + g)),
        ],
        out_specs=pl.BlockSpec((tq, group_w), lambda g, i: (i, g)),
        out_shape=jax.ShapeDtypeStruct((s, n_groups * group_w), BF16),
        scratch_shapes=[pltpu.VMEM((s // tk, HEAD_DIM, tk), BF16),
                        pltpu.VMEM((GQA_REP, HEAD_DIM, tq), BF16),
                        pltpu.VMEM((2, GQA_REP, tk, tq), F32),
                        pltpu.VMEM((2, GQA_REP, 1, tq), F32),
                        pltpu.VMEM((GQA_REP, 1, tq), F32),
                        pltpu.VMEM((GQA_REP, 1, tq), F32),
                        pltpu.VMEM((GQA_REP, HEAD_DIM, tq), F32)],
        compiler_params=_compiler_params(("parallel", "arbitrary")),
    )(qkv, qkv, qkv)


def _attn_b_kernel(q_ref, k_ref, v_ref, rev_ref, o_ref, vt_ref, t_ref, s_ref, smax_ref, *, blk, n_side):
    n_blocks = k_ref.shape[0] // blk
    n_tiles = 2 * n_side + 1
    _store_transposed_blocks(v_ref, vt_ref)

    for jj in range(n_tiles):
        start = (n_tiles - 1 - jj) * blk
        window = jnp.broadcast_to(rev_ref[0, :, start:start + 2 * blk], (blk, 2 * blk))
        t_ref[jj] = pltpu.roll(window, 0, 1, stride=1, stride_axis=0)[:, blk:]
    t_ref[n_tiles] = jnp.full((blk, blk), MASKED_LOGIT, F32)

    def band(i, jj):
        kb = i + (jj - n_side)
        valid = jnp.logical_and(kb >= 0, kb < n_blocks)
        return jnp.clip(kb, 0, n_blocks - 1), jnp.where(valid, jj, n_tiles)

    def q_transposed(i):
        return _transpose_bf16(q_ref[pl.ds(pl.multiple_of(i * blk, blk), blk), :])

    def scores_tile(i, slot, jj, qt, col_max):
        kb, tile = band(i, jj)
        k = k_ref[pl.ds(pl.multiple_of(kb * blk, blk), blk), :]
        s = jnp.dot(k, qt, preferred_element_type=F32) + t_ref[tile]
        s_ref[slot, jj] = s
        tile_max = jnp.max(s, axis=0, keepdims=True)
        return tile_max if col_max is None else jnp.maximum(col_max, tile_max)

    def fold_tile(i, slot, jj, m, acc, denom):
        kb, _ = band(i, jj)
        p = jnp.exp2(s_ref[slot, jj] - m)
        denom = denom + jnp.sum(p, axis=0, keepdims=True)
        acc = acc + jnp.dot(vt_ref[kb], p.astype(BF16), preferred_element_type=F32)
        return acc, denom

    qt = q_transposed(0)
    col_max = None
    for jj in range(n_tiles):
        col_max = scores_tile(0, 0, jj, qt, col_max)
    smax_ref[0] = col_max

    unroll = min(ATT_UNROLL, n_blocks)

    def body(t, carry):
        for u in range(unroll):
            slot = u % 2
            i = unroll * t + u
            nxt = jnp.minimum(i + 1, n_blocks - 1)
            qt = q_transposed(nxt)
            m = smax_ref[slot]
            acc = jnp.zeros((HEAD_DIM, blk), F32)
            denom = jnp.zeros((1, blk), F32)
            col_max = None
            for jj in range(n_tiles):
                col_max = scores_tile(nxt, 1 - slot, jj, qt, col_max)
                acc, denom = fold_tile(i, slot, jj, m, acc, denom)
            smax_ref[1 - slot] = col_max
            o_ref[pl.ds(pl.multiple_of(i * blk, blk), blk), :] = (acc / denom).T.astype(BF16)
        return carry

    lax.fori_loop(0, n_blocks // unroll, body, 0)


def _attention_b(qkv, offset_terms, n_heads, q_col0, k_col0, v_col0):
    s = qkv.shape[0]
    blk = ATT_B_BLK
    n_side = ATT_B_HALO // blk
    n_tiles = 2 * n_side + 1
    assert (s // blk) % min(ATT_UNROLL, s // blk) == 0 and ATT_B_HALO % blk == 0
    q_blk0, k_blk0, v_blk0 = (c // HEAD_DIM for c in (q_col0, k_col0, v_col0))
    return pl.pallas_call(
        functools.partial(_attn_b_kernel, blk=blk, n_side=n_side),
        name="attn_b",
        grid=(n_heads,),
        in_specs=[
            pl.BlockSpec((s, HEAD_DIM), lambda h: (0, q_blk0 + h)),
            pl.BlockSpec((s, HEAD_DIM), lambda h: (0, k_blk0 + h)),
            pl.BlockSpec((s, HEAD_DIM), lambda h: (0, v_blk0 + h)),
            pl.BlockSpec((1, 1, (n_tiles + 1) * blk), lambda h: (h, 0, 0)),
        ],
        out_specs=pl.BlockSpec((s, HEAD_DIM), lambda h: (0, h)),
        out_shape=jax.ShapeDtypeStruct((s, n_heads * HEAD_DIM), BF16),
        scratch_shapes=[pltpu.VMEM((s // blk, HEAD_DIM, blk), BF16),
                        pltpu.VMEM((n_tiles + 1, blk, blk), F32),
                        pltpu.VMEM((2, n_tiles, blk, blk), F32),
                        pltpu.VMEM((2, 1, blk), F32)],
        compiler_params=_compiler_params(("parallel",)),
    )(qkv, qkv, qkv, offset_terms)


def _out_proj_ln_kernel(a_ref, b_ref, w_ref, h_ref, g_ref, beta_ref, o_ref, *, alpha, n_a_chunks):
    k = pl.program_id(1)

    @pl.when(k == 0)
    def _():
        o_ref[...] = alpha * h_ref[...]

    @pl.when(k < n_a_chunks)
    def _():
        o_ref[...] += jnp.dot(a_ref[...], w_ref[...], preferred_element_type=F32)

    @pl.when(k >= n_a_chunks)
    def _():
        o_ref[...] += jnp.dot(b_ref[...], w_ref[...], preferred_element_type=F32)

    @pl.when(k == pl.num_programs(1) - 1)
    def _():
        _layer_norm_in_place(o_ref, g_ref, beta_ref)


def _out_proj_ln(out_a, out_b, w_out, h, ln_g, ln_b, alpha):
    s, d = h.shape
    wa, wb = out_a.shape[1], out_b.shape[1]
    tm = min(OUT_ROWS, s)
    tk = OUT_K
    assert s % tm == 0 and wa % tk == 0 and wb % tk == 0 and tm % LN_CHUNK == 0
    na, nb = wa // tk, wb // tk
    return pl.pallas_call(
        functools.partial(_out_proj_ln_kernel, alpha=alpha, n_a_chunks=na),
        name="out_proj_ln",
        grid=(s // tm, na + nb),
        in_specs=[
            pl.BlockSpec((tm, tk), lambda i, k: (i, jnp.minimum(k, na - 1))),
            pl.BlockSpec((tm, tk), lambda i, k: (i, jnp.maximum(k - na, 0))),
            pl.BlockSpec((tk, d), lambda i, k: (k, 0)),
            pl.BlockSpec((tm, d), lambda i, k: (i, 0)),
            pl.BlockSpec((1, d), lambda i, k: (0, 0)),
            pl.BlockSpec((1, d), lambda i, k: (0, 0)),
        ],
        out_specs=pl.BlockSpec((tm, d), lambda i, k: (i, 0)),
        out_shape=jax.ShapeDtypeStruct((s, d), F32),
        compiler_params=_compiler_params(("parallel", "arbitrary")),
    )(out_a, out_b, w_out, h, ln_g, ln_b)


def _rope_tables(seq_len):
    rows = seq_len // GRID_W
    row = jnp.repeat(jnp.arange(rows, dtype=F32), GRID_W)
    col = jnp.tile(jnp.arange(GRID_W, dtype=F32), rows)
    half = HEAD_DIM // 2
    inv_freq = ROPE_THETA ** (-jnp.arange(0, half, 2, dtype=F32) / half)
    ang = jnp.concatenate([row[:, None] * inv_freq, col[:, None] * inv_freq], axis=-1)
    cos = jnp.repeat(jnp.cos(ang), 2, axis=-1)
    sign = jnp.tile(jnp.array([-1.0, 1.0], F32), HEAD_DIM // 2)
    sin = jnp.repeat(jnp.sin(ang), 2, axis=-1) * sign
    return cos, sin


def _t5_bucket(rel):
    nb = N_REL_BUCKETS // 2
    max_exact = nb // 2
    ret = jnp.where(rel > 0, nb, 0)
    n = jnp.abs(rel)
    large = max_exact + (jnp.log(jnp.maximum(n, 1).astype(F32) / max_exact)
                         / math.log(REL_MAX_DIST / max_exact) * (nb - max_exact)).astype(jnp.int32)
    large = jnp.minimum(large, nb - 1)
    return ret + jnp.where(n < max_exact, n, large)


def _dilated_offset_terms(rel_bias, blk):
    reach = ATT_B_HALO + blk
    off = np.arange(-reach + 1, reach)
    mult = np.zeros(off.shape, np.float64)
    for win, dil in DILATED_CONFIGS:
        mult += (np.abs(off) <= win // 2) & (off % dil == 0)
    log_mult = jnp.asarray(np.log(np.maximum(mult, 1.0)), F32)
    present = jnp.asarray(mult > 0)
    bias = rel_bias[_t5_bucket(jnp.asarray(off, jnp.int32))].astype(F32)
    term = jnp.where(present[:, None], (bias + log_mult[:, None]) * LOG2_E, MASKED_LOGIT).T
    n_heads = term.shape[0]
    return jnp.concatenate([jnp.zeros((n_heads, 1), F32), term[:, ::-1]], axis=1)[:, None, :]


def kernel(x, ffn1_w_gate, ffn1_w_up, ffn1_w_down, ln1_g, ln1_b, w_in, q_norm_g, k_norm_g, rel_bias, w_out,
           ln2_g, ln2_b, ffn2_w_gate, ffn2_w_up, ffn2_w_down, ln3_g, ln3_b):
    batch, s, d = x.shape
    depth = ffn1_w_gate.shape[0]
    assert batch == 1
    n_heads = d // HEAD_DIM
    heads_a = n_heads // 2
    kv_a = heads_a // GQA_REP
    heads_b = n_heads - heads_a
    q_a, kv_a_w, qkv_b = heads_a * HEAD_DIM, kv_a * HEAD_DIM, heads_b * HEAD_DIM
    col_ka, col_va = q_a, q_a + kv_a_w
    col_qb = q_a + 2 * kv_a_w
    col_kb, col_vb = col_qb + qkv_b, col_qb + 2 * qkv_b
    alpha = (2.0 * depth) ** 0.25
    scale = HEAD_DIM ** -0.5 * LOG2_E

    cos_tab, sin_tab = _rope_tables(s)
    offset_terms = _dilated_offset_terms(rel_bias, ATT_B_BLK)

    h = x[0]
    for l in range(depth):
        h = _ffn_ln(h, ffn1_w_gate, ffn1_w_up, ffn1_w_down, l, ln1_g[l][None], ln1_b[l][None], alpha)
        col_gain = jnp.concatenate([
            jnp.tile(q_norm_g[l], heads_a) * scale, jnp.tile(k_norm_g[l], kv_a), jnp.ones((kv_a_w,), F32),
            jnp.full((qkv_b,), scale, F32), jnp.ones((2 * qkv_b,), F32)])[None]
        qkv = _qkv_projection(h, w_in, l, col_gain, cos_tab, sin_tab, col_va)
        out_a = _attention_a(qkv, kv_a, col_ka, col_va)
        out_b = _attention_b(qkv, offset_terms, heads_b, col_qb, col_kb, col_vb)
        h = _out_proj_ln(out_a, out_b, w_out[l].astype(BF16), h, ln2_g[l][None], ln2_b[l][None], alpha)
        h = _ffn_ln(h, ffn2_w_gate, ffn2_w_up, ffn2_w_down, l, ln3_g[l][None], ln3_b[l][None], alpha)
    return h[None]
```

```python
import functools
import math

import jax
import jax.numpy as jnp
import numpy as np
from jax import lax
from jax.experimental import pallas as pl
from jax.experimental.pallas import tpu as pltpu

F32 = jnp.float32
BF16 = jnp.bfloat16

HEAD_DIM = 128
GQA_REP = 4
GRID_W = 64
ROPE_THETA = 10000.0
DILATED_CONFIGS = ((128, 1), (512, 4), (2048, 16))
N_REL_BUCKETS = 32
REL_MAX_DIST = 1024
LN_EPS = 1e-5
RMS_EPS = 1e-6
MASKED_LOGIT = -1e30
LOG2_E = math.log2(math.e)

V7X_LANES = 128
V7X_SCOPED_VMEM_LIMIT_BYTES = 60000 * 1024

FFN_ROWS = 1024
FFN_COLS = 256
QKV_ROWS = 1024
QKV_COLS = 512
OUT_ROWS = 512
OUT_K = 1024
LN_CHUNK = 128
ATT_A_Q = 512
ATT_A_K = 512
ATT_B_BLK = 256
ATT_UNROLL = 8
ATT_B_HALO = max(w for w, _ in DILATED_CONFIGS) // 2


def _compiler_params(semantics):
    return pltpu.CompilerParams(dimension_semantics=semantics,
                                vmem_limit_bytes=V7X_SCOPED_VMEM_LIMIT_BYTES)


def _layer_norm_rows(acc_ref, rows, gain, bias):
    y = acc_ref[rows, :]
    mu = jnp.mean(y, axis=-1, keepdims=True)
    yc = y - mu
    var = jnp.mean(yc * yc, axis=-1, keepdims=True)
    acc_ref[rows, :] = yc * lax.rsqrt(var + LN_EPS) * gain + bias


def _layer_norm_in_place(acc_ref, g_ref, b_ref):
    gain = g_ref[...]
    bias = b_ref[...]

    def body(c, carry):
        _layer_norm_rows(acc_ref, pl.ds(pl.multiple_of(c * LN_CHUNK, LN_CHUNK), LN_CHUNK), gain, bias)
        return carry

    lax.fori_loop(0, acc_ref.shape[0] // LN_CHUNK, body, 0)


def _ffn_ln_kernel(x_hbm, wg_ref, wu_ref, wd_ref, g_ref, b_ref, o_hbm, acc_ref, xb_ref, in_sems, out_sems, *, alpha):
    i = pl.program_id(0)
    j = pl.program_id(1)
    tm = acc_ref.shape[0]
    n_chunks = tm // LN_CHUNK

    def chunk(c):
        return pl.ds(pl.multiple_of(c * LN_CHUNK, LN_CHUNK), LN_CHUNK)

    def hbm_rows(block, c):
        return pl.ds(pl.multiple_of(block * tm + c * LN_CHUNK, LN_CHUNK), LN_CHUNK)

    def load_x(block, c):
        return pltpu.make_async_copy(x_hbm.at[hbm_rows(block, c), :], acc_ref.at[chunk(c), :], in_sems.at[c])

    def store_out(c):
        return pltpu.make_async_copy(acc_ref.at[chunk(c), :], o_hbm.at[hbm_rows(i, c), :], out_sems.at[c])

    def stage(c):
        x = acc_ref[chunk(c), :]
        xb_ref[chunk(c), :] = x.astype(BF16)
        acc_ref[chunk(c), :] = alpha * x

    def for_chunks(n, fn):
        def body(c, carry):
            fn(c)
            return carry
        lax.fori_loop(0, n, body, 0)

    @pl.when(jnp.logical_and(i == 0, j == 0))
    def _():
        for_chunks(n_chunks, lambda c: load_x(0, c).start())

        def arrive(c):
            load_x(0, c).wait()
            stage(c)
        for_chunks(n_chunks, arrive)

    xb = xb_ref[...]
    gate = jnp.dot(xb, wg_ref[...].astype(BF16), preferred_element_type=F32)
    up = jnp.dot(xb, wu_ref[...].astype(BF16), preferred_element_type=F32)
    hidden = (0.5 * gate * jax.nn.sigmoid(gate) * up).astype(BF16)
    acc_ref[...] += jnp.dot(hidden, wd_ref[...].astype(BF16), preferred_element_type=F32)

    is_last_step = j == pl.num_programs(1) - 1
    has_next_block = i + 1 < pl.num_programs(0)

    def normalise_and_store(c):
        _layer_norm_rows(acc_ref, chunk(c), g_ref[...], b_ref[...])
        store_out(c).start()

    @pl.when(jnp.logical_and(is_last_step, has_next_block))
    def _():
        def pipeline(c):
            @pl.when(c < n_chunks)
            def _():
                normalise_and_store(c)

            @pl.when(jnp.logical_and(c >= 1, c <= n_chunks))
            def _():
                store_out(c - 1).wait()
                load_x(i + 1, c - 1).start()

            @pl.when(c >= 2)
            def _():
                load_x(i + 1, c - 2).wait()
                stage(c - 2)
        for_chunks(n_chunks + 2, pipeline)

    @pl.when(jnp.logical_and(is_last_step, jnp.logical_not(has_next_block)))
    def _():
        for_chunks(n_chunks, normalise_and_store)
        for_chunks(n_chunks, lambda c: store_out(c).wait())


def _ffn_ln(x, wg, wu, wd, layer, ln_g, ln_b, alpha):
    s, d = x.shape
    f = wg.shape[2]
    tm = min(FFN_ROWS, s)
    tf = min(FFN_COLS, f)
    assert s % tm == 0 and f % tf == 0 and tm % LN_CHUNK == 0
    return pl.pallas_call(
        functools.partial(_ffn_ln_kernel, alpha=alpha),
        name="ffn_ln",
        grid=(s // tm, f // tf),
        in_specs=[
            pl.BlockSpec(memory_space=pl.ANY),
            pl.BlockSpec((None, d, tf), lambda i, j: (layer, 0, j)),
            pl.BlockSpec((None, d, tf), lambda i, j: (layer, 0, j)),
            pl.BlockSpec((None, tf, d), lambda i, j: (layer, j, 0)),
            pl.BlockSpec((1, d), lambda i, j: (0, 0)),
            pl.BlockSpec((1, d), lambda i, j: (0, 0)),
        ],
        out_specs=pl.BlockSpec(memory_space=pl.ANY),
        out_shape=jax.ShapeDtypeStruct((s, d), F32),
        scratch_shapes=[pltpu.VMEM((tm, d), F32), pltpu.VMEM((tm, d), BF16),
                        pltpu.SemaphoreType.DMA((tm // LN_CHUNK,)), pltpu.SemaphoreType.DMA((tm // LN_CHUNK,))],
        compiler_params=_compiler_params(("arbitrary", "arbitrary")),
    )(x, wg, wu, wd, ln_g, ln_b)


def _qkv_kernel(h_hbm, w_ref, gain_ref, cos_ref, sin_ref, o_ref, hs_ref, hb_ref, raw_ref, h_sem, *, n_rope_tiles):
    i = pl.program_id(0)
    n = pl.program_id(1)
    tm = hs_ref.shape[0]

    def load_h(block):
        rows = pl.ds(pl.multiple_of(block * tm, tm), tm)
        return pltpu.make_async_copy(h_hbm.at[rows, :], hs_ref, h_sem)

    @pl.when(jnp.logical_and(i == 0, n == 0))
    def _():
        load_h(0).start()

    @pl.when(n == 0)
    def _():
        load_h(i).wait()
        hb_ref[...] = hs_ref[...].astype(BF16)

        @pl.when(i + 1 < pl.num_programs(0))
        def _():
            load_h(i + 1).start()

    n_tiles = pl.num_programs(1) - 1

    def project():
        raw_ref[...] = jnp.dot(hb_ref[...], w_ref[...].astype(BF16), preferred_element_type=F32)

    def finish_rope():
        gain = gain_ref[...]
        cos = cos_ref[...]
        sin = sin_ref[...]
        lane = lax.broadcasted_iota(jnp.int32, cos.shape, 1)
        even = (lane & 1) == 0
        for hd in range(raw_ref.shape[1] // HEAD_DIM):
            cols = slice(hd * HEAD_DIM, (hd + 1) * HEAD_DIM)
            xh = raw_ref[:, cols]
            ms = jnp.mean(xh * xh, axis=-1, keepdims=True)
            xn = xh * lax.rsqrt(ms + RMS_EPS) * gain[:, cols]
            partner = jnp.where(even, pltpu.roll(xn, HEAD_DIM - 1, 1), pltpu.roll(xn, 1, 1))
            o_ref[:, cols] = (xn * cos + partner * sin).astype(BF16)

    def finish_plain():
        o_ref[...] = (raw_ref[...] * gain_ref[...]).astype(BF16)

    @pl.when(n == 0)
    def _():
        project()

    @pl.when(jnp.logical_and(n >= 1, n <= n_rope_tiles))
    def _():
        finish_rope()
        project()

    @pl.when(jnp.logical_and(n > n_rope_tiles, n < n_tiles))
    def _():
        finish_plain()
        project()

    @pl.when(n == n_tiles)
    def _():
        finish_plain()


def _qkv_projection(h, w_in, layer, col_gain, cos_tab, sin_tab, n_rope_cols):
    s, d = h.shape
    n_cols = w_in.shape[2]
    tm = min(QKV_ROWS, s)
    tn = QKV_COLS
    n_tiles = n_cols // tn
    assert s % tm == 0 and n_cols % tn == 0 and n_rope_cols % tn == 0 and n_rope_cols < n_cols
    return pl.pallas_call(
        functools.partial(_qkv_kernel, n_rope_tiles=n_rope_cols // tn),
        name="qkv_proj",
        grid=(s // tm, n_tiles + 1),
        in_specs=[
            pl.BlockSpec(memory_space=pl.ANY),
            pl.BlockSpec((None, d, tn), lambda i, n: (layer, 0, jnp.minimum(n, n_tiles - 1))),
            pl.BlockSpec((1, tn), lambda i, n: (0, jnp.maximum(n - 1, 0))),
            pl.BlockSpec((tm, HEAD_DIM), lambda i, n: (i, 0)),
            pl.BlockSpec((tm, HEAD_DIM), lambda i, n: (i, 0)),
        ],
        out_specs=pl.BlockSpec((tm, tn), lambda i, n: (i, jnp.maximum(n - 1, 0))),
        out_shape=jax.ShapeDtypeStruct((s, n_cols), BF16),
        scratch_shapes=[pltpu.VMEM((tm, d), F32), pltpu.VMEM((tm, d), BF16), pltpu.VMEM((tm, tn), F32),
                        pltpu.SemaphoreType.DMA(())],
        compiler_params=_compiler_params(("arbitrary", "arbitrary")),
    )(h, w_in, col_gain, cos_tab, sin_tab)


def _transpose_bf16(x):
    return x.astype(F32).T.astype(BF16)


def _store_transposed_blocks(src_ref, dst_ref):
    n_blocks, _, blk = dst_ref.shape

    def body(c, carry):
        rows = pl.ds(pl.multiple_of(c * blk, blk), blk)
        dst_ref[c] = _transpose_bf16(src_ref[rows, :])
        return carry

    lax.fori_loop(0, n_blocks, body, 0, unroll=min(ATT_UNROLL, n_blocks))


def _attn_a_kernel(q_ref, k_ref, v_ref, o_ref, vt_ref, qt_ref, s_ref, smax_ref, m_ref, l_ref, acc_ref, *, tk):
    n_kblocks = k_ref.shape[0] // tk

    @pl.when(pl.program_id(1) == 0)
    def _():
        _store_transposed_blocks(v_ref, vt_ref)

    for r in range(GQA_REP):
        qt_ref[r] = _transpose_bf16(q_ref[:, r * HEAD_DIM:(r + 1) * HEAD_DIM])
    m_ref[...] = jnp.full_like(m_ref, -jnp.inf)
    l_ref[...] = jnp.zeros_like(l_ref)
    acc_ref[...] = jnp.zeros_like(acc_ref)

    def scores(kb, slot, r):
        k = k_ref[pl.ds(pl.multiple_of(kb * tk, tk), tk), :]
        st = jnp.dot(k, qt_ref[r], preferred_element_type=F32)
        s_ref[slot, r] = st
        smax_ref[slot, r] = jnp.max(st, axis=0, keepdims=True)

    def fold(kb, slot, r):
        m_prev = m_ref[r]
        m_new = jnp.maximum(m_prev, smax_ref[slot, r])
        p = jnp.exp2(s_ref[slot, r] - m_new)
        rescale = jnp.exp2(m_prev - m_new)
        l_ref[r] = rescale * l_ref[r] + jnp.sum(p, axis=0, keepdims=True)
        acc_ref[r] = rescale * acc_ref[r] + jnp.dot(vt_ref[kb], p.astype(BF16), preferred_element_type=F32)
        m_ref[r] = m_new

    for r in range(GQA_REP):
        scores(0, 0, r)

    unroll = min(ATT_UNROLL, n_kblocks)

    def body(t, carry):
        for u in range(unroll):
            slot = u % 2
            kb = unroll * t + u
            nxt = jnp.minimum(kb + 1, n_kblocks - 1)
            for r in range(GQA_REP):
                scores(nxt, 1 - slot, r)
                fold(kb, slot, r)
        return carry

    lax.fori_loop(0, n_kblocks // unroll, body, 0)
    for r in range(GQA_REP):
        o_ref[:, r * HEAD_DIM:(r + 1) * HEAD_DIM] = (acc_ref[r] / l_ref[r]).T.astype(BF16)


def _attention_a(qkv, n_groups, k_col0, v_col0):
    s = qkv.shape[0]
    tq = min(ATT_A_Q, s)
    tk = min(ATT_A_K, s)
    assert s % tq == 0 and (s // tk) % min(ATT_UNROLL, s // tk) == 0
    group_w = GQA_REP * HEAD_DIM
    k_blk0 = k_col0 // HEAD_DIM
    v_blk0 = v_col0 // HEAD_DIM
    return pl.pallas_call(
        functools.partial(_attn_a_kernel, tk=tk),
        name="attn_a",
        grid=(n_groups, s // tq),
        in_specs=[
            pl.BlockSpec((tq, group_w), lambda g, i: (i, g)),
            pl.BlockSpec((s, HEAD_DIM), lambda g, i: (0, k_blk0 + g)),
            pl.BlockSpec((s, HEAD_DIM), lambda g, i: (0, v_blk0 + g)),
        ],
        out_specs=pl.BlockSpec((tq, group_w), lambda g, i: (i, g)),
        out_shape=jax.ShapeDtypeStruct((s, n_groups * group_w), BF16),
        scratch_shapes=[pltpu.VMEM((s // tk, HEAD_DIM, tk), BF16),
                        pltpu.VMEM((GQA_REP, HEAD_DIM, tq), BF16),
                        pltpu.VMEM((2, GQA_REP, tk, tq), F32),
                        pltpu.VMEM((2, GQA_REP, 1, tq), F32),
                        pltpu.VMEM((GQA_REP, 1, tq), F32),
                        pltpu.VMEM((GQA_REP, 1, tq), F32),
                        pltpu.VMEM((GQA_REP, HEAD_DIM, tq), F32)],
        compiler_params=_compiler_params(("parallel", "arbitrary")),
    )(qkv, qkv, qkv)


def _attn_b_kernel(q_ref, k_ref, v_ref, rev_ref, o_ref, vt_ref, t_ref, s_ref, smax_ref, *, blk, n_side):
    n_blocks = k_ref.shape[0] // blk
    n_tiles = 2 * n_side + 1
    _store_transposed_blocks(v_ref, vt_ref)

    for jj in range(n_tiles):
        start = (n_tiles - 1 - jj) * blk
        window = jnp.broadcast_to(rev_ref[0, :, start:start + 2 * blk], (blk, 2 * blk))
        t_ref[jj] = pltpu.roll(window, 0, 1, stride=1, stride_axis=0)[:, blk:]
    t_ref[n_tiles] = jnp.full((blk, blk), MASKED_LOGIT, F32)

    def band(i, jj):
        kb = i + (jj - n_side)
        valid = jnp.logical_and(kb >= 0, kb < n_blocks)
        return jnp.clip(kb, 0, n_blocks - 1), jnp.where(valid, jj, n_tiles)

    def q_transposed(i):
        return _transpose_bf16(q_ref[pl.ds(pl.multiple_of(i * blk, blk), blk), :])

    def scores_tile(i, slot, jj, qt, col_max):
        kb, tile = band(i, jj)
        k = k_ref[pl.ds(pl.multiple_of(kb * blk, blk), blk), :]
        s = jnp.dot(k, qt, preferred_element_type=F32) + t_ref[tile]
        s_ref[slot, jj] = s
        tile_max = jnp.max(s, axis=0, keepdims=True)
        return tile_max if col_max is None else jnp.maximum(col_max, tile_max)

    def fold_tile(i, slot, jj, m, acc, denom):
        kb, _ = band(i, jj)
        p = jnp.exp2(s_ref[slot, jj] - m)
        denom = denom + jnp.sum(p, axis=0, keepdims=True)
        acc = acc + jnp.dot(vt_ref[kb], p.astype(BF16), preferred_element_type=F32)
        return acc, denom

    qt = q_transposed(0)
    col_max = None
    for jj in range(n_tiles):
        col_max = scores_tile(0, 0, jj, qt, col_max)
    smax_ref[0] = col_max

    unroll = min(ATT_UNROLL, n_blocks)

    def body(t, carry):
        for u in range(unroll):
            slot = u % 2
            i = unroll * t + u
            nxt = jnp.minimum(i + 1, n_blocks - 1)
            qt = q_transposed(nxt)
            m = smax_ref[slot]
            acc = jnp.zeros((HEAD_DIM, blk), F32)
            denom = jnp.zeros((1, blk), F32)
            col_max = None
            for jj in range(n_tiles):
                col_max = scores_tile(nxt, 1 - slot, jj, qt, col_max)
                acc, denom = fold_tile(i, slot, jj, m, acc, denom)
            smax_ref[1 - slot] = col_max
            o_ref[pl.ds(pl.multiple_of(i * blk, blk), blk), :] = (acc / denom).T.astype(BF16)
        return carry

    lax.fori_loop(0, n_blocks // unroll, body, 0)


def _attention_b(qkv, offset_terms, n_heads, q_col0, k_col0, v_col0):
    s = qkv.shape[0]
    blk = ATT_B_BLK
    n_side = ATT_B_HALO // blk
    n_tiles = 2 * n_side + 1
    assert (s // blk) % min(ATT_UNROLL, s // blk) == 0 and ATT_B_HALO % blk == 0
    q_blk0, k_blk0, v_blk0 = (c // HEAD_DIM for c in (q_col0, k_col0, v_col0))
    return pl.pallas_call(
        functools.partial(_attn_b_kernel, blk=blk, n_side=n_side),
        name="attn_b",
        grid=(n_heads,),
        in_specs=[
            pl.BlockSpec((s, HEAD_DIM), lambda h: (0, q_blk0 + h)),
            pl.BlockSpec((s, HEAD_DIM), lambda h: (0, k_blk0 + h)),
            pl.BlockSpec((s, HEAD_DIM), lambda h: (0, v_blk0 + h)),
            pl.BlockSpec((1, 1, (n_tiles + 1) * blk), lambda h: (h, 0, 0)),
        ],
        out_specs=pl.BlockSpec((s, HEAD_DIM), lambda h: (0, h)),
        out_shape=jax.ShapeDtypeStruct((s, n_heads * HEAD_DIM), BF16),
        scratch_shapes=[pltpu.VMEM((s // blk, HEAD_DIM, blk), BF16),
                        pltpu.VMEM((n_tiles + 1, blk, blk), F32),
                        pltpu.VMEM((2, n_tiles, blk, blk), F32),
                        pltpu.VMEM((2, 1, blk), F32)],
        compiler_params=_compiler_params(("parallel",)),
    )(qkv, qkv, qkv, offset_terms)


def _out_proj_ln_kernel(a_ref, b_ref, w_ref, h_ref, g_ref, beta_ref, o_ref, *, alpha, n_a_chunks):
    k = pl.program_id(1)

    @pl.when(k == 0)
    def _():
        o_ref[...] = alpha * h_ref[...]

    @pl.when(k < n_a_chunks)
    def _():
        o_ref[...] += jnp.dot(a_ref[...], w_ref[...], preferred_element_type=F32)

    @pl.when(k >= n_a_chunks)
    def _():
        o_ref[...] += jnp.dot(b_ref[...], w_ref[...], preferred_element_type=F32)

    @pl.when(k == pl.num_programs(1) - 1)
    def _():
        _layer_norm_in_place(o_ref, g_ref, beta_ref)


def _out_proj_ln(out_a, out_b, w_out, h, ln_g, ln_b, alpha):
    s, d = h.shape
    wa, wb = out_a.shape[1], out_b.shape[1]
    tm = min(OUT_ROWS, s)
    tk = OUT_K
    assert s % tm == 0 and wa % tk == 0 and wb % tk == 0 and tm % LN_CHUNK == 0
    na, nb = wa // tk, wb // tk
    return pl.pallas_call(
        functools.partial(_out_proj_ln_kernel, alpha=alpha, n_a_chunks=na),
        name="out_proj_ln",
        grid=(s // tm, na + nb),
        in_specs=[
            pl.BlockSpec((tm, tk), lambda i, k: (i, jnp.minimum(k, na - 1))),
            pl.BlockSpec((tm, tk), lambda i, k: (i, jnp.maximum(k - na, 0))),
            pl.BlockSpec((tk, d), lambda i, k: (k, 0)),
            pl.BlockSpec((tm, d), lambda i, k: (i, 0)),
            pl.BlockSpec((1, d), lambda i, k: (0, 0)),
            pl.BlockSpec((1, d), lambda i, k: (0, 0)),
        ],
        out_specs=pl.BlockSpec((tm, d), lambda i, k: (i, 0)),
        out_shape=jax.ShapeDtypeStruct((s, d), F32),
        compiler_params=_compiler_params(("parallel", "arbitrary")),
    )(out_a, out_b, w_out, h, ln_g, ln_b)


def _rope_tables(seq_len):
    rows = seq_len // GRID_W
    row = jnp.repeat(jnp.arange(rows, dtype=F32), GRID_W)
    col = jnp.tile(jnp.arange(GRID_W, dtype=F32), rows)
    half = HEAD_DIM // 2
    inv_freq = ROPE_THETA ** (-jnp.arange(0, half, 2, dtype=F32) / half)
    ang = jnp.concatenate([row[:, None] * inv_freq, col[:, None] * inv_freq], axis=-1)
    cos = jnp.repeat(jnp.cos(ang), 2, axis=-1)
    sign = jnp.tile(jnp.array([-1.0, 1.0], F32), HEAD_DIM // 2)
    sin = jnp.repeat(jnp.sin(ang), 2, axis=-1) * sign
    return cos, sin


def _t5_bucket(rel):
    nb = N_REL_BUCKETS // 2
    max_exact = nb // 2
    ret = jnp.where(rel > 0, nb, 0)
    n = jnp.abs(rel)
    large = max_exact + (jnp.log(jnp.maximum(n, 1).astype(F32) / max_exact)
                         / math.log(REL_MAX_DIST / max_exact) * (nb - max_exact)).astype(jnp.int32)
    large = jnp.minimum(large, nb - 1)
    return ret + jnp.where(n < max_exact, n, large)


def _dilated_offset_terms(rel_bias, blk):
    reach = ATT_B_HALO + blk
    off = np.arange(-reach + 1, reach)
    mult = np.zeros(off.shape, np.float64)
    for win, dil in DILATED_CONFIGS:
        mult += (np.abs(off) <= win // 2) & (off % dil == 0)
    log_mult = jnp.asarray(np.log(np.maximum(mult, 1.0)), F32)
    present = jnp.asarray(mult > 0)
    bias = rel_bias[_t5_bucket(jnp.asarray(off, jnp.int32))].astype(F32)
    term = jnp.where(present[:, None], (bias + log_mult[:, None]) * LOG2_E, MASKED_LOGIT).T
    n_heads = term.shape[0]
    return jnp.concatenate([jnp.zeros((n_heads, 1), F32), term[:, ::-1]], axis=1)[:, None, :]


def kernel(x, ffn1_w_gate, ffn1_w_up, ffn1_w_down, ln1_g, ln1_b, w_in, q_norm_g, k_norm_g, rel_bias, w_out,
           ln2_g, ln2_b, ffn2_w_gate, ffn2_w_up, ffn2_w_down, ln3_g, ln3_b):
    batch, s, d = x.shape
    depth = ffn1_w_gate.shape[0]
    assert batch == 1
    n_heads = d // HEAD_DIM
    heads_a = n_heads // 2
    kv_a = heads_a // GQA_REP
    heads_b = n_heads - heads_a
    q_a, kv_a_w, qkv_b = heads_a * HEAD_DIM, kv_a * HEAD_DIM, heads_b * HEAD_DIM
    col_ka, col_va = q_a, q_a + kv_a_w
    col_qb = q_a + 2 * kv_a_w
    col_kb, col_vb = col_qb + qkv_b, col_qb + 2 * qkv_b
    alpha = (2.0 * depth) ** 0.25
    scale = HEAD_DIM ** -0.5 * LOG2_E

    cos_tab, sin_tab = _rope_tables(s)
    offset_terms = _dilated_offset_terms(rel_bias, ATT_B_BLK)

    h = x[0]
    for l in range(depth):
        h = _ffn_ln(h, ffn1_w_gate, ffn1_w_up, ffn1_w_down, l, ln1_g[l][None], ln1_b[l][None], alpha)
        col_gain = jnp.concatenate([
            jnp.tile(q_norm_g[l], heads_a) * scale, jnp.tile(k_norm_g[l], kv_a), jnp.ones((kv_a_w,), F32),
            jnp.full((qkv_b,), scale, F32), jnp.ones((2 * qkv_b,), F32)])[None]
        qkv = _qkv_projection(h, w_in, l, col_gain, cos_tab, sin_tab, col_va)
        out_a = _attention_a(qkv, kv_a, col_ka, col_va)
        out_b = _attention_b(qkv, offset_terms, heads_b, col_qb, col_kb, col_vb)
        h = _out_proj_ln(out_a, out_b, w_out[l].astype(BF16), h, ln2_g[l][None], ln2_b[l][None], alpha)
        h = _ffn_ln(h, ffn2_w_gate, ffn2_w_up, ffn2_w_down, l, ln3_g[l][None], ln3_b[l][None], alpha)
    return h[None]
```

```python
import functools
import math

import jax
import jax.numpy as jnp
import numpy as np
from jax import lax
from jax.experimental import pallas as pl
from jax.experimental.pallas import tpu as pltpu

F32 = jnp.float32
BF16 = jnp.bfloat16

HEAD_DIM = 128
GQA_REP = 4
GRID_W = 64
ROPE_THETA = 10000.0
DILATED_CONFIGS = ((128, 1), (512, 4), (2048, 16))
N_REL_BUCKETS = 32
REL_MAX_DIST = 1024
LN_EPS = 1e-5
RMS_EPS = 1e-6
MASKED_LOGIT = -1e30
LOG2_E = math.log2(math.e)
DENOM_ROWS = 16

V7X_SCOPED_VMEM_LIMIT_BYTES = 60000 * 1024

FFN_ROWS = 1024
FFN_COLS = 256
QKV_ROWS = 1024
QKV_COLS = 512
OUT_ROWS = 512
OUT_K = 1024
LN_CHUNK = 128
ATT_A_Q = 512
ATT_A_K = 512
ATT_B_BLK = 256
ATT_UNROLL = 8
ATT_B_HALO = max(w for w, _ in DILATED_CONFIGS) // 2


def _compiler_params(semantics):
    return pltpu.CompilerParams(dimension_semantics=semantics,
                                vmem_limit_bytes=V7X_SCOPED_VMEM_LIMIT_BYTES)


def _layer_norm_rows(acc_ref, rows, gain, bias):
    y = acc_ref[rows, :]
    mu = jnp.mean(y, axis=-1, keepdims=True)
    yc = y - mu
    var = jnp.mean(yc * yc, axis=-1, keepdims=True)
    acc_ref[rows, :] = yc * lax.rsqrt(var + LN_EPS) * gain + bias


def _layer_norm_in_place(acc_ref, g_ref, b_ref):
    gain = g_ref[...]
    bias = b_ref[...]

    def body(c, carry):
        _layer_norm_rows(acc_ref, pl.ds(pl.multiple_of(c * LN_CHUNK, LN_CHUNK), LN_CHUNK), gain, bias)
        return carry

    lax.fori_loop(0, acc_ref.shape[0] // LN_CHUNK, body, 0)


def _ffn_ln_kernel(x_hbm, wg_ref, wu_ref, wd_ref, g_ref, b_ref, o_hbm, acc_ref, xb_ref, in_sems, out_sems, *, alpha):
    i = pl.program_id(0)
    j = pl.program_id(1)
    tm = acc_ref.shape[0]
    n_chunks = tm // LN_CHUNK

    def chunk(c):
        return pl.ds(pl.multiple_of(c * LN_CHUNK, LN_CHUNK), LN_CHUNK)

    def hbm_rows(block, c):
        return pl.ds(pl.multiple_of(block * tm + c * LN_CHUNK, LN_CHUNK), LN_CHUNK)

    def load_x(block, c):
        return pltpu.make_async_copy(x_hbm.at[hbm_rows(block, c), :], acc_ref.at[chunk(c), :], in_sems.at[c])

    def store_out(c):
        return pltpu.make_async_copy(acc_ref.at[chunk(c), :], o_hbm.at[hbm_rows(i, c), :], out_sems.at[c])

    def stage(c):
        x = acc_ref[chunk(c), :]
        xb_ref[chunk(c), :] = x.astype(BF16)
        acc_ref[chunk(c), :] = alpha * x

    def for_chunks(n, fn):
        def body(c, carry):
            fn(c)
            return carry
        lax.fori_loop(0, n, body, 0)

    @pl.when(jnp.logical_and(i == 0, j == 0))
    def _():
        for_chunks(n_chunks, lambda c: load_x(0, c).start())

        def arrive(c):
            load_x(0, c).wait()
            stage(c)
        for_chunks(n_chunks, arrive)

    xb = xb_ref[...]
    gate = jnp.dot(xb, wg_ref[...].astype(BF16), preferred_element_type=F32)
    up = jnp.dot(xb, wu_ref[...].astype(BF16), preferred_element_type=F32)
    hidden = (0.5 * gate * jax.nn.sigmoid(gate) * up).astype(BF16)
    acc_ref[...] += jnp.dot(hidden, wd_ref[...].astype(BF16), preferred_element_type=F32)

    is_last_step = j == pl.num_programs(1) - 1
    has_next_block = i + 1 < pl.num_programs(0)

    def normalise_and_store(c):
        _layer_norm_rows(acc_ref, chunk(c), g_ref[...], b_ref[...])
        store_out(c).start()

    @pl.when(jnp.logical_and(is_last_step, has_next_block))
    def _():
        def pipeline(c):
            @pl.when(c < n_chunks)
            def _():
                normalise_and_store(c)

            @pl.when(jnp.logical_and(c >= 1, c <= n_chunks))
            def _():
                store_out(c - 1).wait()
                load_x(i + 1, c - 1).start()

            @pl.when(c >= 2)
            def _():
                load_x(i + 1, c - 2).wait()
                stage(c - 2)
        for_chunks(n_chunks + 2, pipeline)

    @pl.when(jnp.logical_and(is_last_step, jnp.logical_not(has_next_block)))
    def _():
        for_chunks(n_chunks, normalise_and_store)
        for_chunks(n_chunks, lambda c: store_out(c).wait())


def _ffn_ln(x, wg, wu, wd, layer, ln_g, ln_b, alpha):
    s, d = x.shape
    f = wg.shape[2]
    tm = min(FFN_ROWS, s)
    tf = min(FFN_COLS, f)
    assert s % tm == 0 and f % tf == 0 and tm % LN_CHUNK == 0
    return pl.pallas_call(
        functools.partial(_ffn_ln_kernel, alpha=alpha),
        name="ffn_ln",
        grid=(s // tm, f // tf),
        in_specs=[
            pl.BlockSpec(memory_space=pl.ANY),
            pl.BlockSpec((None, d, tf), lambda i, j: (layer, 0, j)),
            pl.BlockSpec((None, d, tf), lambda i, j: (layer, 0, j)),
            pl.BlockSpec((None, tf, d), lambda i, j: (layer, j, 0)),
            pl.BlockSpec((1, d), lambda i, j: (0, 0)),
            pl.BlockSpec((1, d), lambda i, j: (0, 0)),
        ],
        out_specs=pl.BlockSpec(memory_space=pl.ANY),
        out_shape=jax.ShapeDtypeStruct((s, d), F32),
        scratch_shapes=[pltpu.VMEM((tm, d), F32), pltpu.VMEM((tm, d), BF16),
                        pltpu.SemaphoreType.DMA((tm // LN_CHUNK,)), pltpu.SemaphoreType.DMA((tm // LN_CHUNK,))],
        compiler_params=_compiler_params(("arbitrary", "arbitrary")),
    )(x, wg, wu, wd, ln_g, ln_b)


def _qkv_kernel(h_hbm, w_ref, gain_ref, cos_ref, sin_ref, o_ref, hs_ref, hb_ref, h_sem, *, n_rope_tiles):
    i = pl.program_id(0)
    n = pl.program_id(1)
    tm = hs_ref.shape[0]

    def load_h(block):
        rows = pl.ds(pl.multiple_of(block * tm, tm), tm)
        return pltpu.make_async_copy(h_hbm.at[rows, :], hs_ref, h_sem)

    @pl.when(jnp.logical_and(i == 0, n == 0))
    def _():
        load_h(0).start()

    @pl.when(n == 0)
    def _():
        load_h(i).wait()
        hb_ref[...] = hs_ref[...].astype(BF16)

        @pl.when(i + 1 < pl.num_programs(0))
        def _():
            load_h(i + 1).start()

    acc = jnp.dot(hb_ref[...], w_ref[...].astype(BF16), preferred_element_type=F32)
    gain = gain_ref[...]

    @pl.when(n < n_rope_tiles)
    def _():
        cos = cos_ref[...]
        sin = sin_ref[...]
        lane = lax.broadcasted_iota(jnp.int32, cos.shape, 1)
        even = (lane & 1) == 0
        for hd in range(acc.shape[1] // HEAD_DIM):
            cols = slice(hd * HEAD_DIM, (hd + 1) * HEAD_DIM)
            xh = acc[:, cols]
            ms = jnp.mean(xh * xh, axis=-1, keepdims=True)
            xn = xh * lax.rsqrt(ms + RMS_EPS) * gain[:, cols]
            partner = jnp.where(even, pltpu.roll(xn, HEAD_DIM - 1, 1), pltpu.roll(xn, 1, 1))
            o_ref[:, cols] = (xn * cos + partner * sin).astype(BF16)

    @pl.when(n >= n_rope_tiles)
    def _():
        o_ref[...] = (acc * gain).astype(BF16)


def _qkv_projection(h, w_in, layer, col_gain, cos_tab, sin_tab, n_rope_cols):
    s, d = h.shape
    n_cols = w_in.shape[2]
    tm = min(QKV_ROWS, s)
    tn = QKV_COLS
    assert s % tm == 0 and n_cols % tn == 0 and n_rope_cols % tn == 0
    return pl.pallas_call(
        functools.partial(_qkv_kernel, n_rope_tiles=n_rope_cols // tn),
        name="qkv_proj",
        grid=(s // tm, n_cols // tn),
        in_specs=[
            pl.BlockSpec(memory_space=pl.ANY),
            pl.BlockSpec((None, d, tn), lambda i, n: (layer, 0, n)),
            pl.BlockSpec((1, tn), lambda i, n: (0, n)),
            pl.BlockSpec((tm, HEAD_DIM), lambda i, n: (i, 0)),
            pl.BlockSpec((tm, HEAD_DIM), lambda i, n: (i, 0)),
        ],
        out_specs=pl.BlockSpec((tm, tn), lambda i, n: (i, n)),
        out_shape=jax.ShapeDtypeStruct((s, n_cols), BF16),
        scratch_shapes=[pltpu.VMEM((tm, d), F32), pltpu.VMEM((tm, d), BF16), pltpu.SemaphoreType.DMA(())],
        compiler_params=_compiler_params(("arbitrary", "arbitrary")),
    )(h, w_in, col_gain, cos_tab, sin_tab)


def _transpose_bf16(x):
    return x.astype(F32).T.astype(BF16)


def _store_transposed_values(src_ref, dst_ref):
    n_blocks, _, blk = dst_ref.shape

    def body(c, carry):
        rows = pl.ds(pl.multiple_of(c * blk, blk), blk)
        dst_ref[c, :HEAD_DIM, :] = _transpose_bf16(src_ref[rows, :])
        return carry

    lax.fori_loop(0, n_blocks, body, 0, unroll=min(ATT_UNROLL, n_blocks))
    dst_ref[:, HEAD_DIM:, :] = jnp.ones((n_blocks, DENOM_ROWS, blk), BF16)


def _normalised(acc):
    return (acc[:HEAD_DIM] / acc[HEAD_DIM:HEAD_DIM + 1]).T.astype(BF16)


def _attn_a_kernel(q_ref, k_ref, v_ref, o_ref, vt_ref, qt_ref, s_ref, smax_ref, m_ref, acc_ref, *, tk):
    n_kblocks = k_ref.shape[0] // tk

    @pl.when(pl.program_id(1) == 0)
    def _():
        _store_transposed_values(v_ref, vt_ref)

    for r in range(GQA_REP):
        qt_ref[r] = _transpose_bf16(q_ref[:, r * HEAD_DIM:(r + 1) * HEAD_DIM])
    m_ref[...] = jnp.full_like(m_ref, -jnp.inf)
    acc_ref[...] = jnp.zeros_like(acc_ref)

    def scores(kb, slot, r):
        k = k_ref[pl.ds(pl.multiple_of(kb * tk, tk), tk), :]
        st = jnp.dot(k, qt_ref[r], preferred_element_type=F32)
        s_ref[slot, r] = st
        smax_ref[slot, r] = jnp.max(st, axis=0, keepdims=True)

    def fold(kb, slot, r):
        m_prev = m_ref[r]
        m_new = jnp.maximum(m_prev, smax_ref[slot, r])
        p = jnp.exp2(s_ref[slot, r] - m_new)
        rescale = jnp.exp2(m_prev - m_new)
        acc_ref[r] = rescale * acc_ref[r] + jnp.dot(vt_ref[kb], p.astype(BF16), preferred_element_type=F32)
        m_ref[r] = m_new

    for r in range(GQA_REP):
        scores(0, 0, r)

    unroll = min(ATT_UNROLL, n_kblocks)

    def body(t, carry):
        for u in range(unroll):
            slot = u % 2
            kb = unroll * t + u
            nxt = jnp.minimum(kb + 1, n_kblocks - 1)
            for r in range(GQA_REP):
                scores(nxt, 1 - slot, r)
                fold(kb, slot, r)
        return carry

    lax.fori_loop(0, n_kblocks // unroll, body, 0)
    for r in range(GQA_REP):
        o_ref[:, r * HEAD_DIM:(r + 1) * HEAD_DIM] = _normalised(acc_ref[r])


def _attention_a(qkv, n_groups, k_col0, v_col0):
    s = qkv.shape[0]
    tq = min(ATT_A_Q, s)
    tk = min(ATT_A_K, s)
    assert s % tq == 0 and (s // tk) % min(ATT_UNROLL, s // tk) == 0
    group_w = GQA_REP * HEAD_DIM
    k_blk0 = k_col0 // HEAD_DIM
    v_blk0 = v_col0 // HEAD_DIM
    return pl.pallas_call(
        functools.partial(_attn_a_kernel, tk=tk),
        name="attn_a",
        grid=(n_groups, s // tq),
        in_specs=[
            pl.BlockSpec((tq, group_w), lambda g, i: (i, g)),
            pl.BlockSpec((s, HEAD_DIM), lambda g, i: (0, k_blk0 + g)),
            pl.BlockSpec((s, HEAD_DIM), lambda g, i: (0, v_blk0 + g)),
        ],
        out_specs=pl.BlockSpec((tq, group_w), lambda g, i: (i, g)),
        out_shape=jax.ShapeDtypeStruct((s, n_groups * group_w), BF16),
        scratch_shapes=[pltpu.VMEM((s // tk, HEAD_DIM + DENOM_ROWS, tk), BF16),
                        pltpu.VMEM((GQA_REP, HEAD_DIM, tq), BF16),
                        pltpu.VMEM((2, GQA_REP, tk, tq), F32),
                        pltpu.VMEM((2, GQA_REP, 1, tq), F32),
                        pltpu.VMEM((GQA_REP, 1, tq), F32),
                        pltpu.VMEM((GQA_REP, HEAD_DIM + DENOM_ROWS, tq), F32)],
        compiler_params=_compiler_params(("parallel", "arbitrary")),
    )(qkv, qkv, qkv)


def _attn_b_kernel(q_ref, k_ref, v_ref, rev_ref, o_ref, vt_ref, t_ref, s_ref, smax_ref, *, blk, n_side):
    n_blocks = k_ref.shape[0] // blk
    n_tiles = 2 * n_side + 1
    _store_transposed_values(v_ref, vt_ref)

    for jj in range(n_tiles):
        start = (n_tiles - 1 - jj) * blk
        window = jnp.broadcast_to(rev_ref[0, :, start:start + 2 * blk], (blk, 2 * blk))
        t_ref[jj] = pltpu.roll(window, 0, 1, stride=1, stride_axis=0)[:, blk:]
    t_ref[n_tiles] = jnp.full((blk, blk), MASKED_LOGIT, F32)

    def band(i, jj):
        kb = i + (jj - n_side)
        valid = jnp.logical_and(kb >= 0, kb < n_blocks)
        return jnp.clip(kb, 0, n_blocks - 1), jnp.where(valid, jj, n_tiles)

    def q_transposed(i):
        return _transpose_bf16(q_ref[pl.ds(pl.multiple_of(i * blk, blk), blk), :])

    def scores_tile(i, slot, jj, qt, col_max):
        kb, tile = band(i, jj)
        k = k_ref[pl.ds(pl.multiple_of(kb * blk, blk), blk), :]
        s = jnp.dot(k, qt, preferred_element_type=F32) + t_ref[tile]
        s_ref[slot, jj] = s
        tile_max = jnp.max(s, axis=0, keepdims=True)
        return tile_max if col_max is None else jnp.maximum(col_max, tile_max)

    def fold_tile(i, slot, jj, m, acc):
        kb, _ = band(i, jj)
        p = jnp.exp2(s_ref[slot, jj] - m)
        return acc + jnp.dot(vt_ref[kb], p.astype(BF16), preferred_element_type=F32)

    qt = q_transposed(0)
    col_max = None
    for jj in range(n_tiles):
        col_max = scores_tile(0, 0, jj, qt, col_max)
    smax_ref[0] = col_max

    unroll = min(ATT_UNROLL, n_blocks)

    def body(t, carry):
        for u in range(unroll):
            slot = u % 2
            i = unroll * t + u
            nxt = jnp.minimum(i + 1, n_blocks - 1)
            qt = q_transposed(nxt)
            m = smax_ref[slot]
            acc = jnp.zeros((HEAD_DIM + DENOM_ROWS, blk), F32)
            col_max = None
            for jj in range(n_tiles):
                col_max = scores_tile(nxt, 1 - slot, jj, qt, col_max)
                acc = fold_tile(i, slot, jj, m, acc)
            smax_ref[1 - slot] = col_max
            o_ref[pl.ds(pl.multiple_of(i * blk, blk), blk), :] = _normalised(acc)
        return carry

    lax.fori_loop(0, n_blocks // unroll, body, 0)


def _attention_b(qkv, offset_terms, n_heads, q_col0, k_col0, v_col0):
    s = qkv.shape[0]
    blk = ATT_B_BLK
    n_side = ATT_B_HALO // blk
    n_tiles = 2 * n_side + 1
    assert (s // blk) % min(ATT_UNROLL, s // blk) == 0 and ATT_B_HALO % blk == 0
    q_blk0, k_blk0, v_blk0 = (c // HEAD_DIM for c in (q_col0, k_col0, v_col0))
    return pl.pallas_call(
        functools.partial(_attn_b_kernel, blk=blk, n_side=n_side),
        name="attn_b",
        grid=(n_heads,),
        in_specs=[
            pl.BlockSpec((s, HEAD_DIM), lambda h: (0, q_blk0 + h)),
            pl.BlockSpec((s, HEAD_DIM), lambda h: (0, k_blk0 + h)),
            pl.BlockSpec((s, HEAD_DIM), lambda h: (0, v_blk0 + h)),
            pl.BlockSpec((1, 1, (n_tiles + 1) * blk), lambda h: (h, 0, 0)),
        ],
        out_specs=pl.BlockSpec((s, HEAD_DIM), lambda h: (0, h)),
        out_shape=jax.ShapeDtypeStruct((s, n_heads * HEAD_DIM), BF16),
        scratch_shapes=[pltpu.VMEM((s // blk, HEAD_DIM + DENOM_ROWS, blk), BF16),
                        pltpu.VMEM((n_tiles + 1, blk, blk), F32),
                        pltpu.VMEM((2, n_tiles, blk, blk), F32),
                        pltpu.VMEM((2, 1, blk), F32)],
        compiler_params=_compiler_params(("parallel",)),
    )(qkv, qkv, qkv, offset_terms)


def _out_proj_ln_kernel(a_ref, b_ref, w_ref, h_ref, g_ref, beta_ref, o_ref, *, alpha, n_a_chunks):
    k = pl.program_id(1)

    @pl.when(k == 0)
    def _():
        o_ref[...] = alpha * h_ref[...]

    @pl.when(k < n_a_chunks)
    def _():
        o_ref[...] += jnp.dot(a_ref[...], w_ref[...], preferred_element_type=F32)

    @pl.when(k >= n_a_chunks)
    def _():
        o_ref[...] += jnp.dot(b_ref[...], w_ref[...], preferred_element_type=F32)

    @pl.when(k == pl.num_programs(1) - 1)
    def _():
        _layer_norm_in_place(o_ref, g_ref, beta_ref)


def _out_proj_ln(out_a, out_b, w_out, h, ln_g, ln_b, alpha):
    s, d = h.shape
    wa, wb = out_a.shape[1], out_b.shape[1]
    tm = min(OUT_ROWS, s)
    tk = OUT_K
    assert s % tm == 0 and wa % tk == 0 and wb % tk == 0 and tm % LN_CHUNK == 0
    na, nb = wa // tk, wb // tk
    return pl.pallas_call(
        functools.partial(_out_proj_ln_kernel, alpha=alpha, n_a_chunks=na),
        name="out_proj_ln",
        grid=(s // tm, na + nb),
        in_specs=[
            pl.BlockSpec((tm, tk), lambda i, k: (i, jnp.minimum(k, na - 1))),
            pl.BlockSpec((tm, tk), lambda i, k: (i, jnp.maximum(k - na, 0))),
            pl.BlockSpec((tk, d), lambda i, k: (k, 0)),
            pl.BlockSpec((tm, d), lambda i, k: (i, 0)),
            pl.BlockSpec((1, d), lambda i, k: (0, 0)),
            pl.BlockSpec((1, d), lambda i, k: (0, 0)),
        ],
        out_specs=pl.BlockSpec((tm, d), lambda i, k: (i, 0)),
        out_shape=jax.ShapeDtypeStruct((s, d), F32),
        compiler_params=_compiler_params(("parallel", "arbitrary")),
    )(out_a, out_b, w_out, h, ln_g, ln_b)


def _rope_tables(seq_len):
    rows = seq_len // GRID_W
    row = jnp.repeat(jnp.arange(rows, dtype=F32), GRID_W)
    col = jnp.tile(jnp.arange(GRID_W, dtype=F32), rows)
    half = HEAD_DIM // 2
    inv_freq = ROPE_THETA ** (-jnp.arange(0, half, 2, dtype=F32) / half)
    ang = jnp.concatenate([row[:, None] * inv_freq, col[:, None] * inv_freq], axis=-1)
    cos = jnp.repeat(jnp.cos(ang), 2, axis=-1)
    sign = jnp.tile(jnp.array([-1.0, 1.0], F32), HEAD_DIM // 2)
    sin = jnp.repeat(jnp.sin(ang), 2, axis=-1) * sign
    return cos, sin


def _t5_bucket(rel):
    nb = N_REL_BUCKETS // 2
    max_exact = nb // 2
    ret = jnp.where(rel > 0, nb, 0)
    n = jnp.abs(rel)
    large = max_exact + (jnp.log(jnp.maximum(n, 1).astype(F32) / max_exact)
                         / math.log(REL_MAX_DIST / max_exact) * (nb - max_exact)).astype(jnp.int32)
    large = jnp.minimum(large, nb - 1)
    return ret + jnp.where(n < max_exact, n, large)


def _dilated_offset_terms(rel_bias, blk):
    reach = ATT_B_HALO + blk
    off = np.arange(-reach + 1, reach)
    mult = np.zeros(off.shape, np.float64)
    for win, dil in DILATED_CONFIGS:
        mult += (np.abs(off) <= win // 2) & (off % dil == 0)
    log_mult = jnp.asarray(np.log(np.maximum(mult, 1.0)), F32)
    present = jnp.asarray(mult > 0)
    bias = rel_bias[_t5_bucket(jnp.asarray(off, jnp.int32))].astype(F32)
    term = jnp.where(present[:, None], (bias + log_mult[:, None]) * LOG2_E, MASKED_LOGIT).T
    n_heads = term.shape[0]
    return jnp.concatenate([jnp.zeros((n_heads, 1), F32), term[:, ::-1]], axis=1)[:, None, :]


def kernel(x, ffn1_w_gate, ffn1_w_up, ffn1_w_down, ln1_g, ln1_b, w_in, q_norm_g, k_norm_g, rel_bias, w_out,
           ln2_g, ln2_b, ffn2_w_gate, ffn2_w_up, ffn2_w_down, ln3_g, ln3_b):
    batch, s, d = x.shape
    depth = ffn1_w_gate.shape[0]
    assert batch == 1
    n_heads = d // HEAD_DIM
    heads_a = n_heads // 2
    kv_a = heads_a // GQA_REP
    heads_b = n_heads - heads_a
    q_a, kv_a_w, qkv_b = heads_a * HEAD_DIM, kv_a * HEAD_DIM, heads_b * HEAD_DIM
    col_ka, col_va = q_a, q_a + kv_a_w
    col_qb = q_a + 2 * kv_a_w
    col_kb, col_vb = col_qb + qkv_b, col_qb + 2 * qkv_b
    alpha = (2.0 * depth) ** 0.25
    scale = HEAD_DIM ** -0.5 * LOG2_E

    cos_tab, sin_tab = _rope_tables(s)
    offset_terms = _dilated_offset_terms(rel_bias, ATT_B_BLK)

    h = x[0]
    for l in range(depth):
        h = _ffn_ln(h, ffn1_w_gate, ffn1_w_up, ffn1_w_down, l, ln1_g[l][None], ln1_b[l][None], alpha)
        col_gain = jnp.concatenate([
            jnp.tile(q_norm_g[l], heads_a) * scale, jnp.tile(k_norm_g[l], kv_a), jnp.ones((kv_a_w,), F32),
            jnp.full((qkv_b,), scale, F32), jnp.ones((2 * qkv_b,), F32)])[None]
        qkv = _qkv_projection(h, w_in, l, col_gain, cos_tab, sin_tab, col_va)
        out_a = _attention_a(qkv, kv_a, col_ka, col_va)
        out_b = _attention_b(qkv, offset_terms, heads_b, col_qb, col_kb, col_vb)
        h = _out_proj_ln(out_a, out_b, w_out[l].astype(BF16), h, ln2_g[l][None], ln2_b[l][None], alpha)
        h = _ffn_ln(h, ffn2_w_gate, ffn2_w_up, ffn2_w_down, l, ln3_g[l][None], ln3_b[l][None], alpha)
    return h[None]
```

```python
import functools
import math

import jax
import jax.numpy as jnp
import numpy as np
from jax import lax
from jax.experimental import pallas as pl
from jax.experimental.pallas import tpu as pltpu

F32 = jnp.float32
BF16 = jnp.bfloat16

HEAD_DIM = 128
GQA_REP = 4
GRID_W = 64
ROPE_THETA = 10000.0
DILATED_CONFIGS = ((128, 1), (512, 4), (2048, 16))
N_REL_BUCKETS = 32
REL_MAX_DIST = 1024
LN_EPS = 1e-5
RMS_EPS = 1e-6
MASKED_LOGIT = -1e30
LOG2_E = math.log2(math.e)
DENOM_ROWS = 16

V7X_SCOPED_VMEM_LIMIT_BYTES = 60000 * 1024

FFN_ROWS = 1024
FFN_COLS = 256
QKV_ROWS = 1024
QKV_COLS = 512
OUT_ROWS = 512
OUT_K = 1024
LN_CHUNK = 128
ATT_A_Q = 512
ATT_A_K = 512
ATT_B_BLK = 256
ATT_UNROLL = 8
ATT_B_HALO = max(w for w, _ in DILATED_CONFIGS) // 2


def _compiler_params(semantics):
    return pltpu.CompilerParams(dimension_semantics=semantics,
                                vmem_limit_bytes=V7X_SCOPED_VMEM_LIMIT_BYTES)


def _layer_norm_rows(acc_ref, rows, gain, bias):
    y = acc_ref[rows, :]
    mu = jnp.mean(y, axis=-1, keepdims=True)
    yc = y - mu
    var = jnp.mean(yc * yc, axis=-1, keepdims=True)
    acc_ref[rows, :] = yc * lax.rsqrt(var + LN_EPS) * gain + bias


def _layer_norm_in_place(acc_ref, g_ref, b_ref):
    gain = g_ref[...]
    bias = b_ref[...]

    def body(c, carry):
        _layer_norm_rows(acc_ref, pl.ds(pl.multiple_of(c * LN_CHUNK, LN_CHUNK), LN_CHUNK), gain, bias)
        return carry

    lax.fori_loop(0, acc_ref.shape[0] // LN_CHUNK, body, 0)


def _ffn_ln_kernel(x_hbm, wg_hbm, wu_hbm, wd_hbm, g_ref, b_ref, o_hbm, acc_ref, xb_ref, wg_buf, wu_buf, wd_buf,
                   in_sems, out_sems, w_sems, *, alpha, layer, tf):
    i = pl.program_id(0)
    tm = acc_ref.shape[0]
    n_chunks = tm // LN_CHUNK
    n_cols = wg_hbm.shape[2] // tf

    def w_up_copies(c, slot):
        cols = pl.ds(pl.multiple_of(c * tf, tf), tf)
        return (pltpu.make_async_copy(wg_hbm.at[layer, :, cols], wg_buf.at[slot], w_sems.at[slot, 0]),
                pltpu.make_async_copy(wu_hbm.at[layer, :, cols], wu_buf.at[slot], w_sems.at[slot, 1]))

    def w_down_copy(c, slot):
        rows = pl.ds(pl.multiple_of(c * tf, tf), tf)
        return pltpu.make_async_copy(wd_hbm.at[layer, rows, :], wd_buf.at[slot], w_sems.at[slot, 2])

    def start_first_weights():
        for cp in w_up_copies(0, 0):
            cp.start()
        w_down_copy(0, 0).start()
        if n_cols > 1:
            for cp in w_up_copies(1, 1):
                cp.start()

    def ffn_chunk(c, slot, has_next, has_next2):
        xb = xb_ref[...]
        gate = jnp.dot(xb, wg_buf[slot].astype(BF16), preferred_element_type=F32)
        up = jnp.dot(xb, wu_buf[slot].astype(BF16), preferred_element_type=F32)
        hidden = (0.5 * gate * jax.nn.sigmoid(gate) * up).astype(BF16)
        w_down_copy(c, slot).wait()
        if has_next:
            for cp in w_up_copies(c + 1, 1 - slot):
                cp.wait()
        if has_next2:
            for cp in w_up_copies(c + 2, slot):
                cp.start()
        if has_next:
            w_down_copy(c + 1, 1 - slot).start()
        acc_ref[...] += jnp.dot(hidden, wd_buf[slot].astype(BF16), preferred_element_type=F32)

    def chunk(c):
        return pl.ds(pl.multiple_of(c * LN_CHUNK, LN_CHUNK), LN_CHUNK)

    def hbm_rows(block, c):
        return pl.ds(pl.multiple_of(block * tm + c * LN_CHUNK, LN_CHUNK), LN_CHUNK)

    def load_x(block, c):
        return pltpu.make_async_copy(x_hbm.at[hbm_rows(block, c), :], acc_ref.at[chunk(c), :], in_sems.at[c])

    def store_out(c):
        return pltpu.make_async_copy(acc_ref.at[chunk(c), :], o_hbm.at[hbm_rows(i, c), :], out_sems.at[c])

    def stage(c):
        x = acc_ref[chunk(c), :]
        xb_ref[chunk(c), :] = x.astype(BF16)
        acc_ref[chunk(c), :] = alpha * x

    def for_chunks(n, fn):
        def body(c, carry):
            fn(c)
            return carry
        lax.fori_loop(0, n, body, 0)

    @pl.when(i == 0)
    def _():
        start_first_weights()
        for_chunks(n_chunks, lambda c: load_x(0, c).start())

        def arrive(c):
            load_x(0, c).wait()
            stage(c)
        for_chunks(n_chunks, arrive)

    for cp in w_up_copies(0, 0):
        cp.wait()

    n_pairs = max(0, (n_cols - 2) // 2)

    def pair(p, carry):
        ffn_chunk(2 * p, 0, True, True)
        ffn_chunk(2 * p + 1, 1, True, True)
        return carry

    lax.fori_loop(0, n_pairs, pair, 0)
    for c in range(2 * n_pairs, n_cols):
        ffn_chunk(c, c % 2, c + 1 < n_cols, c + 2 < n_cols)

    has_next_block = i + 1 < pl.num_programs(0)

    def normalise_and_store(c):
        _layer_norm_rows(acc_ref, chunk(c), g_ref[...], b_ref[...])
        store_out(c).start()

    @pl.when(has_next_block)
    def _():
        start_first_weights()

        def pipeline(c):
            @pl.when(c < n_chunks)
            def _():
                normalise_and_store(c)

            @pl.when(jnp.logical_and(c >= 1, c <= n_chunks))
            def _():
                store_out(c - 1).wait()
                load_x(i + 1, c - 1).start()

            @pl.when(c >= 2)
            def _():
                load_x(i + 1, c - 2).wait()
                stage(c - 2)
        for_chunks(n_chunks + 2, pipeline)

    @pl.when(jnp.logical_not(has_next_block))
    def _():
        for_chunks(n_chunks, normalise_and_store)
        for_chunks(n_chunks, lambda c: store_out(c).wait())


def _ffn_ln(x, wg, wu, wd, layer, ln_g, ln_b, alpha):
    s, d = x.shape
    f = wg.shape[2]
    tm = min(FFN_ROWS, s)
    tf = min(FFN_COLS, f)
    assert s % tm == 0 and f % tf == 0 and tm % LN_CHUNK == 0
    return pl.pallas_call(
        functools.partial(_ffn_ln_kernel, alpha=alpha, layer=layer, tf=tf),
        name="ffn_ln",
        grid=(s // tm,),
        in_specs=[
            pl.BlockSpec(memory_space=pl.ANY),
            pl.BlockSpec(memory_space=pl.ANY),
            pl.BlockSpec(memory_space=pl.ANY),
            pl.BlockSpec(memory_space=pl.ANY),
            pl.BlockSpec((1, d), lambda i: (0, 0)),
            pl.BlockSpec((1, d), lambda i: (0, 0)),
        ],
        out_specs=pl.BlockSpec(memory_space=pl.ANY),
        out_shape=jax.ShapeDtypeStruct((s, d), F32),
        scratch_shapes=[pltpu.VMEM((tm, d), F32), pltpu.VMEM((tm, d), BF16),
                        pltpu.VMEM((2, d, tf), F32), pltpu.VMEM((2, d, tf), F32), pltpu.VMEM((2, tf, d), F32),
                        pltpu.SemaphoreType.DMA((tm // LN_CHUNK,)), pltpu.SemaphoreType.DMA((tm // LN_CHUNK,)),
                        pltpu.SemaphoreType.DMA((2, 3))],
        compiler_params=_compiler_params(("arbitrary",)),
    )(x, wg, wu, wd, ln_g, ln_b)


def _qkv_kernel(h_hbm, w_ref, gain_ref, cos_ref, sin_ref, o_ref, hs_ref, hb_ref, h_sem, *, n_rope_tiles):
    i = pl.program_id(0)
    n = pl.program_id(1)
    tm = hs_ref.shape[0]

    def load_h(block):
        rows = pl.ds(pl.multiple_of(block * tm, tm), tm)
        return pltpu.make_async_copy(h_hbm.at[rows, :], hs_ref, h_sem)

    @pl.when(jnp.logical_and(i == 0, n == 0))
    def _():
        load_h(0).start()

    @pl.when(n == 0)
    def _():
        load_h(i).wait()
        hb_ref[...] = hs_ref[...].astype(BF16)

        @pl.when(i + 1 < pl.num_programs(0))
        def _():
            load_h(i + 1).start()

    acc = jnp.dot(hb_ref[...], w_ref[...].astype(BF16), preferred_element_type=F32)
    gain = gain_ref[...]

    @pl.when(n < n_rope_tiles)
    def _():
        cos = cos_ref[...]
        sin = sin_ref[...]
        lane = lax.broadcasted_iota(jnp.int32, cos.shape, 1)
        even = (lane & 1) == 0
        for hd in range(acc.shape[1] // HEAD_DIM):
            cols = slice(hd * HEAD_DIM, (hd + 1) * HEAD_DIM)
            xh = acc[:, cols]
            ms = jnp.mean(xh * xh, axis=-1, keepdims=True)
            xn = xh * lax.rsqrt(ms + RMS_EPS) * gain[:, cols]
            partner = jnp.where(even, pltpu.roll(xn, HEAD_DIM - 1, 1), pltpu.roll(xn, 1, 1))
            o_ref[:, cols] = (xn * cos + partner * sin).astype(BF16)

    @pl.when(n >= n_rope_tiles)
    def _():
        o_ref[...] = (acc * gain).astype(BF16)


def _qkv_projection(h, w_in, layer, col_gain, cos_tab, sin_tab, n_rope_cols):
    s, d = h.shape
    n_cols = w_in.shape[2]
    tm = min(QKV_ROWS, s)
    tn = QKV_COLS
    assert s % tm == 0 and n_cols % tn == 0 and n_rope_cols % tn == 0
    return pl.pallas_call(
        functools.partial(_qkv_kernel, n_rope_tiles=n_rope_cols // tn),
        name="qkv_proj",
        grid=(s // tm, n_cols // tn),
        in_specs=[
            pl.BlockSpec(memory_space=pl.ANY),
            pl.BlockSpec((None, d, tn), lambda i, n: (layer, 0, n)),
            pl.BlockSpec((1, tn), lambda i, n: (0, n)),
            pl.BlockSpec((tm, HEAD_DIM), lambda i, n: (i, 0)),
            pl.BlockSpec((tm, HEAD_DIM), lambda i, n: (i, 0)),
        ],
        out_specs=pl.BlockSpec((tm, tn), lambda i, n: (i, n)),
        out_shape=jax.ShapeDtypeStruct((s, n_cols), BF16),
        scratch_shapes=[pltpu.VMEM((tm, d), F32), pltpu.VMEM((tm, d), BF16), pltpu.SemaphoreType.DMA(())],
        compiler_params=_compiler_params(("arbitrary", "arbitrary")),
    )(h, w_in, col_gain, cos_tab, sin_tab)


def _transpose_bf16(x):
    return x.astype(F32).T.astype(BF16)


def _store_transposed_values(src_ref, dst_ref):
    n_blocks, _, blk = dst_ref.shape

    def body(c, carry):
        rows = pl.ds(pl.multiple_of(c * blk, blk), blk)
        dst_ref[c, :HEAD_DIM, :] = _transpose_bf16(src_ref[rows, :])
        return carry

    lax.fori_loop(0, n_blocks, body, 0, unroll=min(ATT_UNROLL, n_blocks))
    dst_ref[:, HEAD_DIM:, :] = jnp.ones((n_blocks, DENOM_ROWS, blk), BF16)


def _normalised(acc):
    return (acc[:HEAD_DIM] / acc[HEAD_DIM:HEAD_DIM + 1]).T.astype(BF16)


def _attn_a_kernel(q_ref, k_ref, v_ref, o_ref, vt_ref, qt_ref, s_ref, smax_ref, m_ref, acc_ref, *, tk):
    n_kblocks = k_ref.shape[0] // tk

    @pl.when(pl.program_id(1) == 0)
    def _():
        _store_transposed_values(v_ref, vt_ref)

    for r in range(GQA_REP):
        qt_ref[r] = _transpose_bf16(q_ref[:, r * HEAD_DIM:(r + 1) * HEAD_DIM])
    m_ref[...] = jnp.full_like(m_ref, -jnp.inf)
    acc_ref[...] = jnp.zeros_like(acc_ref)

    def scores(kb, slot, r):
        k = k_ref[pl.ds(pl.multiple_of(kb * tk, tk), tk), :]
        st = jnp.dot(k, qt_ref[r], preferred_element_type=F32)
        s_ref[slot, r] = st
        smax_ref[slot, r] = jnp.max(st, axis=0, keepdims=True)

    def fold(kb, slot, r):
        m_prev = m_ref[r]
        m_new = jnp.maximum(m_prev, smax_ref[slot, r])
        p = jnp.exp2(s_ref[slot, r] - m_new)
        rescale = jnp.exp2(m_prev - m_new)
        acc_ref[r] = rescale * acc_ref[r] + jnp.dot(vt_ref[kb], p.astype(BF16), preferred_element_type=F32)
        m_ref[r] = m_new

    for r in range(GQA_REP):
        scores(0, 0, r)

    unroll = min(ATT_UNROLL, n_kblocks)

    def body(t, carry):
        for u in range(unroll):
            slot = u % 2
            kb = unroll * t + u
            nxt = jnp.minimum(kb + 1, n_kblocks - 1)
            for r in range(GQA_REP):
                scores(nxt, 1 - slot, r)
                fold(kb, slot, r)
        return carry

    lax.fori_loop(0, n_kblocks // unroll, body, 0)
    for r in range(GQA_REP):
        o_ref[:, r * HEAD_DIM:(r + 1) * HEAD_DIM] = _normalised(acc_ref[r])


def _attention_a(qkv, n_groups, k_col0, v_col0):
    s = qkv.shape[0]
    tq = min(ATT_A_Q, s)
    tk = min(ATT_A_K, s)
    assert s % tq == 0 and (s // tk) % min(ATT_UNROLL, s // tk) == 0
    group_w = GQA_REP * HEAD_DIM
    k_blk0 = k_col0 // HEAD_DIM
    v_blk0 = v_col0 // HEAD_DIM
    return pl.pallas_call(
        functools.partial(_attn_a_kernel, tk=tk),
        name="attn_a",
        grid=(n_groups, s // tq),
        in_specs=[
            pl.BlockSpec((tq, group_w), lambda g, i: (i, g)),
            pl.BlockSpec((s, HEAD_DIM), lambda g, i: (0, k_blk0 + g)),
            pl.BlockSpec((s, HEAD_DIM), lambda g, i: (0, v_blk0 + g)),
        ],
        out_specs=pl.BlockSpec((tq, group_w), lambda g, i: (i, g)),
        out_shape=jax.ShapeDtypeStruct((s, n_groups * group_w), BF16),
        scratch_shapes=[pltpu.VMEM((s // tk, HEAD_DIM + DENOM_ROWS, tk), BF16),
                        pltpu.VMEM((GQA_REP, HEAD_DIM, tq), BF16),
                        pltpu.VMEM((2, GQA_REP, tk, tq), F32),
                        pltpu.VMEM((2, GQA_REP, 1, tq), F32),
                        pltpu.VMEM((GQA_REP, 1, tq), F32),
                        pltpu.VMEM((GQA_REP, HEAD_DIM + DENOM_ROWS, tq), F32)],
        compiler_params=_compiler_params(("parallel", "arbitrary")),
    )(qkv, qkv, qkv)


def _attn_b_kernel(q_ref, k_ref, v_ref, rev_ref, o_ref, vt_ref, t_ref, s_ref, smax_ref, *, blk, n_side):
    n_blocks = k_ref.shape[0] // blk
    n_tiles = 2 * n_side + 1
    _store_transposed_values(v_ref, vt_ref)

    for jj in range(n_tiles):
        start = (n_tiles - 1 - jj) * blk
        window = jnp.broadcast_to(rev_ref[0, :, start:start + 2 * blk], (blk, 2 * blk))
        t_ref[jj] = pltpu.roll(window, 0, 1, stride=1, stride_axis=0)[:, blk:]
    t_ref[n_tiles] = jnp.full((blk, blk), MASKED_LOGIT, F32)

    def band(i, jj):
        kb = i + (jj - n_side)
        valid = jnp.logical_and(kb >= 0, kb < n_blocks)
        return jnp.clip(kb, 0, n_blocks - 1), jnp.where(valid, jj, n_tiles)

    def q_transposed(i):
        return _transpose_bf16(q_ref[pl.ds(pl.multiple_of(i * blk, blk), blk), :])

    def scores_tile(i, slot, jj, qt, col_max):
        kb, tile = band(i, jj)
        k = k_ref[pl.ds(pl.multiple_of(kb * blk, blk), blk), :]
        s = jnp.dot(k, qt, preferred_element_type=F32) + t_ref[tile]
        s_ref[slot, jj] = s
        tile_max = jnp.max(s, axis=0, keepdims=True)
        return tile_max if col_max is None else jnp.maximum(col_max, tile_max)

    def fold_tile(i, slot, jj, m, acc):
        kb, _ = band(i, jj)
        p = jnp.exp2(s_ref[slot, jj] - m)
        return acc + jnp.dot(vt_ref[kb], p.astype(BF16), preferred_element_type=F32)

    qt = q_transposed(0)
    col_max = None
    for jj in range(n_tiles):
        col_max = scores_tile(0, 0, jj, qt, col_max)
    smax_ref[0] = col_max

    unroll = min(ATT_UNROLL, n_blocks)

    def body(t, carry):
        for u in range(unroll):
            slot = u % 2
            i = unroll * t + u
            nxt = jnp.minimum(i + 1, n_blocks - 1)
            qt = q_transposed(nxt)
            m = smax_ref[slot]
            acc = jnp.zeros((HEAD_DIM + DENOM_ROWS, blk), F32)
            col_max = None
            for jj in range(n_tiles):
                col_max = scores_tile(nxt, 1 - slot, jj, qt, col_max)
                acc = fold_tile(i, slot, jj, m, acc)
            smax_ref[1 - slot] = col_max
            o_ref[pl.ds(pl.multiple_of(i * blk, blk), blk), :] = _normalised(acc)
        return carry

    lax.fori_loop(0, n_blocks // unroll, body, 0)


def _attention_b(qkv, offset_terms, n_heads, q_col0, k_col0, v_col0):
    s = qkv.shape[0]
    blk = ATT_B_BLK
    n_side = ATT_B_HALO // blk
    n_tiles = 2 * n_side + 1
    assert (s // blk) % min(ATT_UNROLL, s // blk) == 0 and ATT_B_HALO % blk == 0
    q_blk0, k_blk0, v_blk0 = (c // HEAD_DIM for c in (q_col0, k_col0, v_col0))
    return pl.pallas_call(
        functools.partial(_attn_b_kernel, blk=blk, n_side=n_side),
        name="attn_b",
        grid=(n_heads,),
        in_specs=[
            pl.BlockSpec((s, HEAD_DIM), lambda h: (0, q_blk0 + h)),
            pl.BlockSpec((s, HEAD_DIM), lambda h: (0, k_blk0 + h)),
            pl.BlockSpec((s, HEAD_DIM), lambda h: (0, v_blk0 + h)),
            pl.BlockSpec((1, 1, (n_tiles + 1) * blk), lambda h: (h, 0, 0)),
        ],
        out_specs=pl.BlockSpec((s, HEAD_DIM), lambda h: (0, h)),
        out_shape=jax.ShapeDtypeStruct((s, n_heads * HEAD_DIM), BF16),
        scratch_shapes=[pltpu.VMEM((s // blk, HEAD_DIM + DENOM_ROWS, blk), BF16),
                        pltpu.VMEM((n_tiles + 1, blk, blk), F32),
                        pltpu.VMEM((2, n_tiles, blk, blk), F32),
                        pltpu.VMEM((2, 1, blk), F32)],
        compiler_params=_compiler_params(("parallel",)),
    )(qkv, qkv, qkv, offset_terms)


def _out_proj_ln_kernel(a_ref, b_ref, w_ref, h_ref, g_ref, beta_ref, o_ref, *, alpha, n_a_chunks):
    k = pl.program_id(1)

    @pl.when(k == 0)
    def _():
        o_ref[...] = alpha * h_ref[...]

    @pl.when(k < n_a_chunks)
    def _():
        o_ref[...] += jnp.dot(a_ref[...], w_ref[...], preferred_element_type=F32)

    @pl.when(k >= n_a_chunks)
    def _():
        o_ref[...] += jnp.dot(b_ref[...], w_ref[...], preferred_element_type=F32)

    @pl.when(k == pl.num_programs(1) - 1)
    def _():
        _layer_norm_in_place(o_ref, g_ref, beta_ref)


def _out_proj_ln(out_a, out_b, w_out, h, ln_g, ln_b, alpha):
    s, d = h.shape
    wa, wb = out_a.shape[1], out_b.shape[1]
    tm = min(OUT_ROWS, s)
    tk = OUT_K
    assert s % tm == 0 and wa % tk == 0 and wb % tk == 0 and tm % LN_CHUNK == 0
    na, nb = wa // tk, wb // tk
    return pl.pallas_call(
        functools.partial(_out_proj_ln_kernel, alpha=alpha, n_a_chunks=na),
        name="out_proj_ln",
        grid=(s // tm, na + nb),
        in_specs=[
            pl.BlockSpec((tm, tk), lambda i, k: (i, jnp.minimum(k, na - 1))),
            pl.BlockSpec((tm, tk), lambda i, k: (i, jnp.maximum(k - na, 0))),
            pl.BlockSpec((tk, d), lambda i, k: (k, 0)),
            pl.BlockSpec((tm, d), lambda i, k: (i, 0)),
            pl.BlockSpec((1, d), lambda i, k: (0, 0)),
            pl.BlockSpec((1, d), lambda i, k: (0, 0)),
        ],
        out_specs=pl.BlockSpec((tm, d), lambda i, k: (i, 0)),
        out_shape=jax.ShapeDtypeStruct((s, d), F32),
        compiler_params=_compiler_params(("parallel", "arbitrary")),
    )(out_a, out_b, w_out, h, ln_g, ln_b)


def _rope_tables(seq_len):
    rows = seq_len // GRID_W
    row = jnp.repeat(jnp.arange(rows, dtype=F32), GRID_W)
    col = jnp.tile(jnp.arange(GRID_W, dtype=F32), rows)
    half = HEAD_DIM // 2
    inv_freq = ROPE_THETA ** (-jnp.arange(0, half, 2, dtype=F32) / half)
    ang = jnp.concatenate([row[:, None] * inv_freq, col[:, None] * inv_freq], axis=-1)
    cos = jnp.repeat(jnp.cos(ang), 2, axis=-1)
    sign = jnp.tile(jnp.array([-1.0, 1.0], F32), HEAD_DIM // 2)
    sin = jnp.repeat(jnp.sin(ang), 2, axis=-1) * sign
    return cos, sin


def _t5_bucket(rel):
    nb = N_REL_BUCKETS // 2
    max_exact = nb // 2
    ret = jnp.where(rel > 0, nb, 0)
    n = jnp.abs(rel)
    large = max_exact + (jnp.log(jnp.maximum(n, 1).astype(F32) / max_exact)
                         / math.log(REL_MAX_DIST / max_exact) * (nb - max_exact)).astype(jnp.int32)
    large = jnp.minimum(large, nb - 1)
    return ret + jnp.where(n < max_exact, n, large)


def _dilated_offset_terms(rel_bias, blk):
    reach = ATT_B_HALO + blk
    off = np.arange(-reach + 1, reach)
    mult = np.zeros(off.shape, np.float64)
    for win, dil in DILATED_CONFIGS:
        mult += (np.abs(off) <= win // 2) & (off % dil == 0)
    log_mult = jnp.asarray(np.log(np.maximum(mult, 1.0)), F32)
    present = jnp.asarray(mult > 0)
    bias = rel_bias[_t5_bucket(jnp.asarray(off, jnp.int32))].astype(F32)
    term = jnp.where(present[:, None], (bias + log_mult[:, None]) * LOG2_E, MASKED_LOGIT).T
    n_heads = term.shape[0]
    return jnp.concatenate([jnp.zeros((n_heads, 1), F32), term[:, ::-1]], axis=1)[:, None, :]


def kernel(x, ffn1_w_gate, ffn1_w_up, ffn1_w_down, ln1_g, ln1_b, w_in, q_norm_g, k_norm_g, rel_bias, w_out,
           ln2_g, ln2_b, ffn2_w_gate, ffn2_w_up, ffn2_w_down, ln3_g, ln3_b):
    batch, s, d = x.shape
    depth = ffn1_w_gate.shape[0]
    assert batch == 1
    n_heads = d // HEAD_DIM
    heads_a = n_heads // 2
    kv_a = heads_a // GQA_REP
    heads_b = n_heads - heads_a
    q_a, kv_a_w, qkv_b = heads_a * HEAD_DIM, kv_a * HEAD_DIM, heads_b * HEAD_DIM
    col_ka, col_va = q_a, q_a + kv_a_w
    col_qb = q_a + 2 * kv_a_w
    col_kb, col_vb = col_qb + qkv_b, col_qb + 2 * qkv_b
    alpha = (2.0 * depth) ** 0.25
    scale = HEAD_DIM ** -0.5 * LOG2_E

    cos_tab, sin_tab = _rope_tables(s)
    offset_terms = _dilated_offset_terms(rel_bias, ATT_B_BLK)

    h = x[0]
    for l in range(depth):
        h = _ffn_ln(h, ffn1_w_gate, ffn1_w_up, ffn1_w_down, l, ln1_g[l][None], ln1_b[l][None], alpha)
        col_gain = jnp.concatenate([
            jnp.tile(q_norm_g[l], heads_a) * scale, jnp.tile(k_norm_g[l], kv_a), jnp.ones((kv_a_w,), F32),
            jnp.full((qkv_b,), scale, F32), jnp.ones((2 * qkv_b,), F32)])[None]
        qkv = _qkv_projection(h, w_in, l, col_gain, cos_tab, sin_tab, col_va)
        out_a = _attention_a(qkv, kv_a, col_ka, col_va)
        out_b = _attention_b(qkv, offset_terms, heads_b, col_qb, col_kb, col_vb)
        h = _out_proj_ln(out_a, out_b, w_out[l].astype(BF16), h, ln2_g[l][None], ln2_b[l][None], alpha)
        h = _ffn_ln(h, ffn2_w_gate, ffn2_w_up, ffn2_w_down, l, ln3_g[l][None], ln3_b[l][None], alpha)
    return h[None]
```

```python
import functools
import math

import jax
import jax.numpy as jnp
import numpy as np
from jax import lax
from jax.experimental import pallas as pl
from jax.experimental.pallas import tpu as pltpu

F32 = jnp.float32
BF16 = jnp.bfloat16

HEAD_DIM = 128
GQA_REP = 4
GRID_W = 64
ROPE_THETA = 10000.0
DILATED_CONFIGS = ((128, 1), (512, 4), (2048, 16))
N_REL_BUCKETS = 32
REL_MAX_DIST = 1024
LN_EPS = 1e-5
RMS_EPS = 1e-6
MASKED_LOGIT = -1e30
LOG2_E = math.log2(math.e)
DENOM_ROWS = 16

V7X_SCOPED_VMEM_LIMIT_BYTES = 60000 * 1024

FFN_ROWS = 1024
FFN_COLS = 256
QKV_ROWS = 1024
QKV_COLS = 512
OUT_ROWS = 512
OUT_K = 1024
LN_CHUNK = 128
ATT_A_Q = 512
ATT_A_K = 512
ATT_B_BLK = 256
ATT_UNROLL = 8
ATT_B_HALO = max(w for w, _ in DILATED_CONFIGS) // 2


def _compiler_params(semantics):
    return pltpu.CompilerParams(dimension_semantics=semantics,
                                vmem_limit_bytes=V7X_SCOPED_VMEM_LIMIT_BYTES)


def _layer_norm_rows(acc_ref, rows, gain, bias, eps=LN_EPS):
    y = acc_ref[rows, :]
    mu = jnp.mean(y, axis=-1, keepdims=True)
    yc = y - mu
    var = jnp.mean(yc * yc, axis=-1, keepdims=True)
    acc_ref[rows, :] = yc * lax.rsqrt(var + eps) * gain + bias


def _layer_norm_in_place(acc_ref, g_ref, b_ref):
    gain = g_ref[...]
    bias = b_ref[...]

    def body(c, carry):
        _layer_norm_rows(acc_ref, pl.ds(pl.multiple_of(c * LN_CHUNK, LN_CHUNK), LN_CHUNK), gain, bias)
        return carry

    lax.fori_loop(0, acc_ref.shape[0] // LN_CHUNK, body, 0)


def _ffn_ln_kernel(x_hbm, wg_ref, wu_ref, wd_ref, g_ref, b_ref, o_hbm, acc_ref, xb_ref, in_sems, out_sems, *, alpha):
    i = pl.program_id(0)
    j = pl.program_id(1)
    tm = acc_ref.shape[0]
    n_chunks = tm // LN_CHUNK

    def chunk(c):
        return pl.ds(pl.multiple_of(c * LN_CHUNK, LN_CHUNK), LN_CHUNK)

    def hbm_rows(block, c):
        return pl.ds(pl.multiple_of(block * tm + c * LN_CHUNK, LN_CHUNK), LN_CHUNK)

    def load_x(block, c):
        return pltpu.make_async_copy(x_hbm.at[hbm_rows(block, c), :], acc_ref.at[chunk(c), :], in_sems.at[c])

    def store_out(c):
        return pltpu.make_async_copy(acc_ref.at[chunk(c), :], o_hbm.at[hbm_rows(i, c), :], out_sems.at[c])

    def stage(c):
        xb_ref[chunk(c), :] = acc_ref[chunk(c), :].astype(BF16)

    def for_chunks(n, fn):
        def body(c, carry):
            fn(c)
            return carry
        lax.fori_loop(0, n, body, 0)

    @pl.when(jnp.logical_and(i == 0, j == 0))
    def _():
        for_chunks(n_chunks, lambda c: load_x(0, c).start())

        def arrive(c):
            load_x(0, c).wait()
            stage(c)
        for_chunks(n_chunks, arrive)

    xb = xb_ref[...]
    gate = jnp.dot(xb, wg_ref[...].astype(BF16), preferred_element_type=F32)
    up = jnp.dot(xb, wu_ref[...].astype(BF16), preferred_element_type=F32)
    hidden = ((0.5 / alpha) * gate * jax.nn.sigmoid(gate) * up).astype(BF16)
    acc_ref[...] += jnp.dot(hidden, wd_ref[...].astype(BF16), preferred_element_type=F32)

    is_last_step = j == pl.num_programs(1) - 1
    has_next_block = i + 1 < pl.num_programs(0)

    def normalise_and_store(c):
        _layer_norm_rows(acc_ref, chunk(c), g_ref[...], b_ref[...], eps=LN_EPS / (alpha * alpha))
        store_out(c).start()

    @pl.when(jnp.logical_and(is_last_step, has_next_block))
    def _():
        def pipeline(c):
            @pl.when(c < n_chunks)
            def _():
                normalise_and_store(c)

            @pl.when(jnp.logical_and(c >= 1, c <= n_chunks))
            def _():
                store_out(c - 1).wait()
                load_x(i + 1, c - 1).start()

            @pl.when(c >= 2)
            def _():
                load_x(i + 1, c - 2).wait()
                stage(c - 2)
        for_chunks(n_chunks + 2, pipeline)

    @pl.when(jnp.logical_and(is_last_step, jnp.logical_not(has_next_block)))
    def _():
        for_chunks(n_chunks, normalise_and_store)
        for_chunks(n_chunks, lambda c: store_out(c).wait())


def _ffn_ln(x, wg, wu, wd, layer, ln_g, ln_b, alpha):
    s, d = x.shape
    f = wg.shape[2]
    tm = min(FFN_ROWS, s)
    tf = min(FFN_COLS, f)
    assert s % tm == 0 and f % tf == 0 and tm % LN_CHUNK == 0
    return pl.pallas_call(
        functools.partial(_ffn_ln_kernel, alpha=alpha),
        name="ffn_ln",
        grid=(s // tm, f // tf),
        in_specs=[
            pl.BlockSpec(memory_space=pl.ANY),
            pl.BlockSpec((None, d, tf), lambda i, j: (layer, 0, j)),
            pl.BlockSpec((None, d, tf), lambda i, j: (layer, 0, j)),
            pl.BlockSpec((None, tf, d), lambda i, j: (layer, j, 0)),
            pl.BlockSpec((1, d), lambda i, j: (0, 0)),
            pl.BlockSpec((1, d), lambda i, j: (0, 0)),
        ],
        out_specs=pl.BlockSpec(memory_space=pl.ANY),
        out_shape=jax.ShapeDtypeStruct((s, d), F32),
        scratch_shapes=[pltpu.VMEM((tm, d), F32), pltpu.VMEM((tm, d), BF16),
                        pltpu.SemaphoreType.DMA((tm // LN_CHUNK,)), pltpu.SemaphoreType.DMA((tm // LN_CHUNK,))],
        compiler_params=_compiler_params(("arbitrary", "arbitrary")),
    )(x, wg, wu, wd, ln_g, ln_b)


def _qkv_kernel(h_hbm, w_ref, gain_ref, cos_ref, sin_ref, o_ref, hs_ref, hb_ref, h_sem, *, n_rope_tiles):
    i = pl.program_id(0)
    n = pl.program_id(1)
    tm = hs_ref.shape[0]

    def load_h(block):
        rows = pl.ds(pl.multiple_of(block * tm, tm), tm)
        return pltpu.make_async_copy(h_hbm.at[rows, :], hs_ref, h_sem)

    @pl.when(jnp.logical_and(i == 0, n == 0))
    def _():
        load_h(0).start()

    @pl.when(n == 0)
    def _():
        load_h(i).wait()
        hb_ref[...] = hs_ref[...].astype(BF16)

        @pl.when(i + 1 < pl.num_programs(0))
        def _():
            load_h(i + 1).start()

    acc = jnp.dot(hb_ref[...], w_ref[...].astype(BF16), preferred_element_type=F32)
    gain = gain_ref[...]

    @pl.when(n < n_rope_tiles)
    def _():
        cos = cos_ref[...]
        sin = sin_ref[...]
        lane = lax.broadcasted_iota(jnp.int32, cos.shape, 1)
        even = (lane & 1) == 0
        for hd in range(acc.shape[1] // HEAD_DIM):
            cols = slice(hd * HEAD_DIM, (hd + 1) * HEAD_DIM)
            xh = acc[:, cols]
            ms = jnp.mean(xh * xh, axis=-1, keepdims=True)
            xn = xh * lax.rsqrt(ms + RMS_EPS) * gain[:, cols]
            partner = jnp.where(even, pltpu.roll(xn, HEAD_DIM - 1, 1), pltpu.roll(xn, 1, 1))
            o_ref[:, cols] = (xn * cos + partner * sin).astype(BF16)

    @pl.when(n >= n_rope_tiles)
    def _():
        o_ref[...] = (acc * gain).astype(BF16)


def _qkv_projection(h, w_in, layer, col_gain, cos_tab, sin_tab, n_rope_cols):
    s, d = h.shape
    n_cols = w_in.shape[2]
    tm = min(QKV_ROWS, s)
    tn = QKV_COLS
    assert s % tm == 0 and n_cols % tn == 0 and n_rope_cols % tn == 0
    return pl.pallas_call(
        functools.partial(_qkv_kernel, n_rope_tiles=n_rope_cols // tn),
        name="qkv_proj",
        grid=(s // tm, n_cols // tn),
        in_specs=[
            pl.BlockSpec(memory_space=pl.ANY),
            pl.BlockSpec((None, d, tn), lambda i, n: (layer, 0, n)),
            pl.BlockSpec((1, tn), lambda i, n: (0, n)),
            pl.BlockSpec((tm, HEAD_DIM), lambda i, n: (i, 0)),
            pl.BlockSpec((tm, HEAD_DIM), lambda i, n: (i, 0)),
        ],
        out_specs=pl.BlockSpec((tm, tn), lambda i, n: (i, n)),
        out_shape=jax.ShapeDtypeStruct((s, n_cols), BF16),
        scratch_shapes=[pltpu.VMEM((tm, d), F32), pltpu.VMEM((tm, d), BF16), pltpu.SemaphoreType.DMA(())],
        compiler_params=_compiler_params(("arbitrary", "arbitrary")),
    )(h, w_in, col_gain, cos_tab, sin_tab)


def _transpose_bf16(x):
    return x.astype(F32).T.astype(BF16)


def _store_transposed_values(src_ref, dst_ref):
    n_blocks, _, blk = dst_ref.shape

    def body(c, carry):
        rows = pl.ds(pl.multiple_of(c * blk, blk), blk)
        dst_ref[c, :HEAD_DIM, :] = _transpose_bf16(src_ref[rows, :])
        return carry

    lax.fori_loop(0, n_blocks, body, 0, unroll=min(ATT_UNROLL, n_blocks))
    dst_ref[:, HEAD_DIM:, :] = jnp.ones((n_blocks, DENOM_ROWS, blk), BF16)


def _normalised(acc):
    return (acc[:HEAD_DIM] / acc[HEAD_DIM:HEAD_DIM + 1]).T.astype(BF16)


def _attn_a_kernel(q_ref, k_ref, v_ref, o_ref, vt_ref, qt_ref, s_ref, smax_ref, m_ref, acc_ref, *, tk):
    n_kblocks = k_ref.shape[0] // tk

    @pl.when(pl.program_id(1) == 0)
    def _():
        _store_transposed_values(v_ref, vt_ref)

    for r in range(GQA_REP):
        qt_ref[r] = _transpose_bf16(q_ref[:, r * HEAD_DIM:(r + 1) * HEAD_DIM])
    m_ref[...] = jnp.full_like(m_ref, -jnp.inf)
    acc_ref[...] = jnp.zeros_like(acc_ref)

    def scores(kb, slot, r):
        k = k_ref[pl.ds(pl.multiple_of(kb * tk, tk), tk), :]
        st = jnp.dot(k, qt_ref[r], preferred_element_type=F32)
        s_ref[slot, r] = st
        smax_ref[slot, r] = jnp.max(st, axis=0, keepdims=True)

    def fold(kb, slot, r):
        m_prev = m_ref[r]
        m_new = jnp.maximum(m_prev, smax_ref[slot, r])
        p = jnp.exp2(s_ref[slot, r] - m_new)
        rescale = jnp.exp2(m_prev - m_new)
        acc_ref[r] = rescale * acc_ref[r] + jnp.dot(vt_ref[kb], p.astype(BF16), preferred_element_type=F32)
        m_ref[r] = m_new

    for r in range(GQA_REP):
        scores(0, 0, r)

    unroll = min(ATT_UNROLL, n_kblocks)

    def body(t, carry):
        for u in range(unroll):
            slot = u % 2
            kb = unroll * t + u
            nxt = jnp.minimum(kb + 1, n_kblocks - 1)
            for r in range(GQA_REP):
                scores(nxt, 1 - slot, r)
                fold(kb, slot, r)
        return carry

    lax.fori_loop(0, n_kblocks // unroll, body, 0)
    for r in range(GQA_REP):
        o_ref[:, r * HEAD_DIM:(r + 1) * HEAD_DIM] = _normalised(acc_ref[r])


def _attention_a(qkv, n_groups, k_col0, v_col0):
    s = qkv.shape[0]
    tq = min(ATT_A_Q, s)
    tk = min(ATT_A_K, s)
    assert s % tq == 0 and (s // tk) % min(ATT_UNROLL, s // tk) == 0
    group_w = GQA_REP * HEAD_DIM
    k_blk0 = k_col0 // HEAD_DIM
    v_blk0 = v_col0 // HEAD_DIM
    return pl.pallas_call(
        functools.partial(_attn_a_kernel, tk=tk),
        name="attn_a",
        grid=(n_groups, s // tq),
        in_specs=[
            pl.BlockSpec((tq, group_w), lambda g, i: (i, g)),
            pl.BlockSpec((s, HEAD_DIM), lambda g, i: (0, k_blk0 + g)),
            pl.BlockSpec((s, HEAD_DIM), lambda g, i: (0, v_blk0 + g)),
        ],
        out_specs=pl.BlockSpec((tq, group_w), lambda g, i: (i, g)),
        out_shape=jax.ShapeDtypeStruct((s, n_groups * group_w), BF16),
        scratch_shapes=[pltpu.VMEM((s // tk, HEAD_DIM + DENOM_ROWS, tk), BF16),
                        pltpu.VMEM((GQA_REP, HEAD_DIM, tq), BF16),
                        pltpu.VMEM((2, GQA_REP, tk, tq), F32),
                        pltpu.VMEM((2, GQA_REP, 1, tq), F32),
                        pltpu.VMEM((GQA_REP, 1, tq), F32),
                        pltpu.VMEM((GQA_REP, HEAD_DIM + DENOM_ROWS, tq), F32)],
        compiler_params=_compiler_params(("parallel", "arbitrary")),
    )(qkv, qkv, qkv)


def _attn_b_kernel(q_ref, k_ref, v_ref, rev_ref, o_ref, vt_ref, t_ref, s_ref, smax_ref, *, blk, n_side):
    n_blocks = k_ref.shape[0] // blk
    n_tiles = 2 * n_side + 1
    _store_transposed_values(v_ref, vt_ref)

    for jj in range(n_tiles):
        start = (n_tiles - 1 - jj) * blk
        window = jnp.broadcast_to(rev_ref[0, :, start:start + 2 * blk], (blk, 2 * blk))
        t_ref[jj] = pltpu.roll(window, 0, 1, stride=1, stride_axis=0)[:, blk:]
    t_ref[n_tiles] = jnp.full((blk, blk), MASKED_LOGIT, F32)

    def band(i, jj):
        kb = i + (jj - n_side)
        valid = jnp.logical_and(kb >= 0, kb < n_blocks)
        return jnp.clip(kb, 0, n_blocks - 1), jnp.where(valid, jj, n_tiles)

    def q_transposed(i):
        return _transpose_bf16(q_ref[pl.ds(pl.multiple_of(i * blk, blk), blk), :])

    def scores_tile(i, slot, jj, qt, col_max):
        kb, tile = band(i, jj)
        k = k_ref[pl.ds(pl.multiple_of(kb * blk, blk), blk), :]
        s = jnp.dot(k, qt, preferred_element_type=F32) + t_ref[tile]
        s_ref[slot, jj] = s
        tile_max = jnp.max(s, axis=0, keepdims=True)
        return tile_max if col_max is None else jnp.maximum(col_max, tile_max)

    def fold_tile(i, slot, jj, m, acc):
        kb, _ = band(i, jj)
        p = jnp.exp2(s_ref[slot, jj] - m)
        return acc + jnp.dot(vt_ref[kb], p.astype(BF16), preferred_element_type=F32)

    qt = q_transposed(0)
    col_max = None
    for jj in range(n_tiles):
        col_max = scores_tile(0, 0, jj, qt, col_max)
    smax_ref[0] = col_max

    unroll = min(ATT_UNROLL, n_blocks)

    def body(t, carry):
        for u in range(unroll):
            slot = u % 2
            i = unroll * t + u
            nxt = jnp.minimum(i + 1, n_blocks - 1)
            qt = q_transposed(nxt)
            m = smax_ref[slot]
            acc = jnp.zeros((HEAD_DIM + DENOM_ROWS, blk), F32)
            col_max = None
            for jj in range(n_tiles):
                col_max = scores_tile(nxt, 1 - slot, jj, qt, col_max)
                acc = fold_tile(i, slot, jj, m, acc)
            smax_ref[1 - slot] = col_max
            o_ref[pl.ds(pl.multiple_of(i * blk, blk), blk), :] = _normalised(acc)
        return carry

    lax.fori_loop(0, n_blocks // unroll, body, 0)


def _attention_b(qkv, offset_terms, n_heads, q_col0, k_col0, v_col0):
    s = qkv.shape[0]
    blk = ATT_B_BLK
    n_side = ATT_B_HALO // blk
    n_tiles = 2 * n_side + 1
    assert (s // blk) % min(ATT_UNROLL, s // blk) == 0 and ATT_B_HALO % blk == 0
    q_blk0, k_blk0, v_blk0 = (c // HEAD_DIM for c in (q_col0, k_col0, v_col0))
    return pl.pallas_call(
        functools.partial(_attn_b_kernel, blk=blk, n_side=n_side),
        name="attn_b",
        grid=(n_heads,),
        in_specs=[
            pl.BlockSpec((s, HEAD_DIM), lambda h: (0, q_blk0 + h)),
            pl.BlockSpec((s, HEAD_DIM), lambda h: (0, k_blk0 + h)),
            pl.BlockSpec((s, HEAD_DIM), lambda h: (0, v_blk0 + h)),
            pl.BlockSpec((1, 1, (n_tiles + 1) * blk), lambda h: (h, 0, 0)),
        ],
        out_specs=pl.BlockSpec((s, HEAD_DIM), lambda h: (0, h)),
        out_shape=jax.ShapeDtypeStruct((s, n_heads * HEAD_DIM), BF16),
        scratch_shapes=[pltpu.VMEM((s // blk, HEAD_DIM + DENOM_ROWS, blk), BF16),
                        pltpu.VMEM((n_tiles + 1, blk, blk), F32),
                        pltpu.VMEM((2, n_tiles, blk, blk), F32),
                        pltpu.VMEM((2, 1, blk), F32)],
        compiler_params=_compiler_params(("parallel",)),
    )(qkv, qkv, qkv, offset_terms)


def _out_proj_ln_kernel(a_ref, b_ref, w_ref, h_ref, g_ref, beta_ref, o_ref, *, alpha, n_a_chunks):
    k = pl.program_id(1)

    @pl.when(k == 0)
    def _():
        o_ref[...] = alpha * h_ref[...]

    @pl.when(k < n_a_chunks)
    def _():
        o_ref[...] += jnp.dot(a_ref[...], w_ref[...], preferred_element_type=F32)

    @pl.when(k >= n_a_chunks)
    def _():
        o_ref[...] += jnp.dot(b_ref[...], w_ref[...], preferred_element_type=F32)

    @pl.when(k == pl.num_programs(1) - 1)
    def _():
        _layer_norm_in_place(o_ref, g_ref, beta_ref)


def _out_proj_ln(out_a, out_b, w_out, h, ln_g, ln_b, alpha):
    s, d = h.shape
    wa, wb = out_a.shape[1], out_b.shape[1]
    tm = min(OUT_ROWS, s)
    tk = OUT_K
    assert s % tm == 0 and wa % tk == 0 and wb % tk == 0 and tm % LN_CHUNK == 0
    na, nb = wa // tk, wb // tk
    return pl.pallas_call(
        functools.partial(_out_proj_ln_kernel, alpha=alpha, n_a_chunks=na),
        name="out_proj_ln",
        grid=(s // tm, na + nb),
        in_specs=[
            pl.BlockSpec((tm, tk), lambda i, k: (i, jnp.minimum(k, na - 1))),
            pl.BlockSpec((tm, tk), lambda i, k: (i, jnp.maximum(k - na, 0))),
            pl.BlockSpec((tk, d), lambda i, k: (k, 0)),
            pl.BlockSpec((tm, d), lambda i, k: (i, 0)),
            pl.BlockSpec((1, d), lambda i, k: (0, 0)),
            pl.BlockSpec((1, d), lambda i, k: (0, 0)),
        ],
        out_specs=pl.BlockSpec((tm, d), lambda i, k: (i, 0)),
        out_shape=jax.ShapeDtypeStruct((s, d), F32),
        compiler_params=_compiler_params(("parallel", "arbitrary")),
    )(out_a, out_b, w_out, h, ln_g, ln_b)


def _rope_tables(seq_len):
    rows = seq_len // GRID_W
    row = jnp.repeat(jnp.arange(rows, dtype=F32), GRID_W)
    col = jnp.tile(jnp.arange(GRID_W, dtype=F32), rows)
    half = HEAD_DIM // 2
    inv_freq = ROPE_THETA ** (-jnp.arange(0, half, 2, dtype=F32) / half)
    ang = jnp.concatenate([row[:, None] * inv_freq, col[:, None] * inv_freq], axis=-1)
    cos = jnp.repeat(jnp.cos(ang), 2, axis=-1)
    sign = jnp.tile(jnp.array([-1.0, 1.0], F32), HEAD_DIM // 2)
    sin = jnp.repeat(jnp.sin(ang), 2, axis=-1) * sign
    return cos, sin


def _t5_bucket(rel):
    nb = N_REL_BUCKETS // 2
    max_exact = nb // 2
    ret = jnp.where(rel > 0, nb, 0)
    n = jnp.abs(rel)
    large = max_exact + (jnp.log(jnp.maximum(n, 1).astype(F32) / max_exact)
                         / math.log(REL_MAX_DIST / max_exact) * (nb - max_exact)).astype(jnp.int32)
    large = jnp.minimum(large, nb - 1)
    return ret + jnp.where(n < max_exact, n, large)


def _dilated_offset_terms(rel_bias, blk):
    reach = ATT_B_HALO + blk
    off = np.arange(-reach + 1, reach)
    mult = np.zeros(off.shape, np.float64)
    for win, dil in DILATED_CONFIGS:
        mult += (np.abs(off) <= win // 2) & (off % dil == 0)
    log_mult = jnp.asarray(np.log(np.maximum(mult, 1.0)), F32)
    present = jnp.asarray(mult > 0)
    bias = rel_bias[_t5_bucket(jnp.asarray(off, jnp.int32))].astype(F32)
    term = jnp.where(present[:, None], (bias + log_mult[:, None]) * LOG2_E, MASKED_LOGIT).T
    n_heads = term.shape[0]
    return jnp.concatenate([jnp.zeros((n_heads, 1), F32), term[:, ::-1]], axis=1)[:, None, :]


def kernel(x, ffn1_w_gate, ffn1_w_up, ffn1_w_down, ln1_g, ln1_b, w_in, q_norm_g, k_norm_g, rel_bias, w_out,
           ln2_g, ln2_b, ffn2_w_gate, ffn2_w_up, ffn2_w_down, ln3_g, ln3_b):
    batch, s, d = x.shape
    depth = ffn1_w_gate.shape[0]
    assert batch == 1
    n_heads = d // HEAD_DIM
    heads_a = n_heads // 2
    kv_a = heads_a // GQA_REP
    heads_b = n_heads - heads_a
    q_a, kv_a_w, qkv_b = heads_a * HEAD_DIM, kv_a * HEAD_DIM, heads_b * HEAD_DIM
    col_ka, col_va = q_a, q_a + kv_a_w
    col_qb = q_a + 2 * kv_a_w
    col_kb, col_vb = col_qb + qkv_b, col_qb + 2 * qkv_b
    alpha = (2.0 * depth) ** 0.25
    scale = HEAD_DIM ** -0.5 * LOG2_E

    cos_tab, sin_tab = _rope_tables(s)
    offset_terms = _dilated_offset_terms(rel_bias, ATT_B_BLK)

    h = x[0]
    for l in range(depth):
        h = _ffn_ln(h, ffn1_w_gate, ffn1_w_up, ffn1_w_down, l, ln1_g[l][None], ln1_b[l][None], alpha)
        col_gain = jnp.concatenate([
            jnp.tile(q_norm_g[l], heads_a) * scale, jnp.tile(k_norm_g[l], kv_a), jnp.ones((kv_a_w,), F32),
            jnp.full((qkv_b,), scale, F32), jnp.ones((2 * qkv_b,), F32)])[None]
        qkv = _qkv_projection(h, w_in, l, col_gain, cos_tab, sin_tab, col_va)
        out_a = _attention_a(qkv, kv_a, col_ka, col_va)
        out_b = _attention_b(qkv, offset_terms, heads_b, col_qb, col_kb, col_vb)
        h = _out_proj_ln(out_a, out_b, w_out[l].astype(BF16), h, ln2_g[l][None], ln2_b[l][None], alpha)
        h = _ffn_ln(h, ffn2_w_gate, ffn2_w_up, ffn2_w_down, l, ln3_g[l][None], ln3_b[l][None], alpha)
    return h[None]
```

```python
import functools
import math

import jax
import jax.numpy as jnp
import numpy as np
from jax import lax
from jax.experimental import pallas as pl
from jax.experimental.pallas import tpu as pltpu

F32 = jnp.float32
BF16 = jnp.bfloat16

HEAD_DIM = 128
GQA_REP = 4
GRID_W = 64
ROPE_THETA = 10000.0
DILATED_CONFIGS = ((128, 1), (512, 4), (2048, 16))
N_REL_BUCKETS = 32
REL_MAX_DIST = 1024
LN_EPS = 1e-5
RMS_EPS = 1e-6
MASKED_LOGIT = -1e30
LOG2_E = math.log2(math.e)
DENOM_ROWS = 16

V7X_SCOPED_VMEM_LIMIT_BYTES = 60000 * 1024

FFN_ROWS = 1024
FFN_COLS = 256
QKV_ROWS = 1024
QKV_COLS = 512
OUT_ROWS = 512
OUT_K = 1024
LN_CHUNK = 128
ATT_A_Q = 512
ATT_A_K = 512
ATT_B_BLK = 256
ATT_UNROLL = 8
ATT_B_HALO = max(w for w, _ in DILATED_CONFIGS) // 2


def _compiler_params(semantics):
    return pltpu.CompilerParams(dimension_semantics=semantics,
                                vmem_limit_bytes=V7X_SCOPED_VMEM_LIMIT_BYTES)


def _layer_norm_rows(acc_ref, rows, gain, bias, eps=LN_EPS):
    y = acc_ref[rows, :]
    mu = jnp.mean(y, axis=-1, keepdims=True)
    yc = y - mu
    var = jnp.mean(yc * yc, axis=-1, keepdims=True)
    acc_ref[rows, :] = yc * lax.rsqrt(var + eps) * gain + bias


def _layer_norm_in_place(acc_ref, g_ref, b_ref):
    gain = g_ref[...]
    bias = b_ref[...]

    def body(c, carry):
        _layer_norm_rows(acc_ref, pl.ds(pl.multiple_of(c * LN_CHUNK, LN_CHUNK), LN_CHUNK), gain, bias)
        return carry

    lax.fori_loop(0, acc_ref.shape[0] // LN_CHUNK, body, 0)


def _ffn_ln_kernel(x_hbm, wg_ref, wu_ref, wd_ref, g_ref, b_ref, o_hbm, acc_ref, xb_ref, in_sems, out_sems, *, alpha):
    i = pl.program_id(0)
    j = pl.program_id(1)
    tm = acc_ref.shape[0]
    n_chunks = tm // LN_CHUNK

    def chunk(c):
        return pl.ds(pl.multiple_of(c * LN_CHUNK, LN_CHUNK), LN_CHUNK)

    def hbm_rows(block, c):
        return pl.ds(pl.multiple_of(block * tm + c * LN_CHUNK, LN_CHUNK), LN_CHUNK)

    def load_x(block, c):
        return pltpu.make_async_copy(x_hbm.at[hbm_rows(block, c), :], acc_ref.at[chunk(c), :], in_sems.at[c])

    def store_out(c):
        return pltpu.make_async_copy(acc_ref.at[chunk(c), :], o_hbm.at[hbm_rows(i, c), :], out_sems.at[c])

    def stage(c):
        xb_ref[chunk(c), :] = acc_ref[chunk(c), :].astype(BF16)

    def for_chunks(n, fn):
        def body(c, carry):
            fn(c)
            return carry
        lax.fori_loop(0, n, body, 0)

    @pl.when(jnp.logical_and(i == 0, j == 0))
    def _():
        for_chunks(n_chunks, lambda c: load_x(0, c).start())

        def arrive(c):
            load_x(0, c).wait()
            stage(c)
        for_chunks(n_chunks, arrive)

    xb = xb_ref[...]
    gate = jnp.dot(xb, wg_ref[...].astype(BF16), preferred_element_type=F32)
    up = jnp.dot(xb, wu_ref[...].astype(BF16), preferred_element_type=F32)
    hidden = ((0.5 / alpha) * gate * jax.nn.sigmoid(gate) * up).astype(BF16)
    acc_ref[...] += jnp.dot(hidden, wd_ref[...].astype(BF16), preferred_element_type=F32)

    is_last_step = j == pl.num_programs(1) - 1
    has_next_block = i + 1 < pl.num_programs(0)

    def normalise_and_store(c):
        _layer_norm_rows(acc_ref, chunk(c), g_ref[...], b_ref[...], eps=LN_EPS / (alpha * alpha))
        store_out(c).start()

    @pl.when(jnp.logical_and(is_last_step, has_next_block))
    def _():
        def pipeline(c):
            @pl.when(c < n_chunks)
            def _():
                normalise_and_store(c)

            @pl.when(jnp.logical_and(c >= 1, c <= n_chunks))
            def _():
                store_out(c - 1).wait()
                load_x(i + 1, c - 1).start()

            @pl.when(c >= 2)
            def _():
                load_x(i + 1, c - 2).wait()
                stage(c - 2)
        for_chunks(n_chunks + 2, pipeline)

    @pl.when(jnp.logical_and(is_last_step, jnp.logical_not(has_next_block)))
    def _():
        for_chunks(n_chunks, normalise_and_store)
        for_chunks(n_chunks, lambda c: store_out(c).wait())


def _ffn_ln(x, wg, wu, wd, layer, ln_g, ln_b, alpha):
    s, d = x.shape
    f = wg.shape[2]
    tm = min(FFN_ROWS, s)
    tf = min(FFN_COLS, f)
    assert s % tm == 0 and f % tf == 0 and tm % LN_CHUNK == 0
    return pl.pallas_call(
        functools.partial(_ffn_ln_kernel, alpha=alpha),
        name="ffn_ln",
        grid=(s // tm, f // tf),
        in_specs=[
            pl.BlockSpec(memory_space=pl.ANY),
            pl.BlockSpec((None, d, tf), lambda i, j: (layer, 0, j)),
            pl.BlockSpec((None, d, tf), lambda i, j: (layer, 0, j)),
            pl.BlockSpec((None, tf, d), lambda i, j: (layer, j, 0)),
            pl.BlockSpec((1, d), lambda i, j: (0, 0)),
            pl.BlockSpec((1, d), lambda i, j: (0, 0)),
        ],
        out_specs=pl.BlockSpec(memory_space=pl.ANY),
        out_shape=jax.ShapeDtypeStruct((s, d), F32),
        scratch_shapes=[pltpu.VMEM((tm, d), F32), pltpu.VMEM((tm, d), BF16),
                        pltpu.SemaphoreType.DMA((tm // LN_CHUNK,)), pltpu.SemaphoreType.DMA((tm // LN_CHUNK,))],
        compiler_params=_compiler_params(("arbitrary", "arbitrary")),
    )(x, wg, wu, wd, ln_g, ln_b)


def _qkv_kernel(h_hbm, w_ref, gain_ref, cos_ref, sin_ref, o_ref, hs_ref, hb_ref, h_sem, *, n_rope_tiles):
    i = pl.program_id(0)
    n = pl.program_id(1)
    tm = hs_ref.shape[0]

    def load_h(block):
        rows = pl.ds(pl.multiple_of(block * tm, tm), tm)
        return pltpu.make_async_copy(h_hbm.at[rows, :], hs_ref, h_sem)

    @pl.when(jnp.logical_and(i == 0, n == 0))
    def _():
        load_h(0).start()

    @pl.when(n == 0)
    def _():
        load_h(i).wait()
        hb_ref[...] = hs_ref[...].astype(BF16)

        @pl.when(i + 1 < pl.num_programs(0))
        def _():
            load_h(i + 1).start()

    acc = jnp.dot(hb_ref[...], w_ref[...].astype(BF16), preferred_element_type=F32)
    gain = gain_ref[...]

    @pl.when(n < n_rope_tiles)
    def _():
        cos = cos_ref[...]
        sin = sin_ref[...]
        lane = lax.broadcasted_iota(jnp.int32, cos.shape, 1)
        even = (lane & 1) == 0
        for hd in range(acc.shape[1] // HEAD_DIM):
            cols = slice(hd * HEAD_DIM, (hd + 1) * HEAD_DIM)
            xh = acc[:, cols]
            ms = jnp.mean(xh * xh, axis=-1, keepdims=True)
            xn = xh * lax.rsqrt(ms + RMS_EPS) * gain[:, cols]
            partner = jnp.where(even, pltpu.roll(xn, HEAD_DIM - 1, 1), pltpu.roll(xn, 1, 1))
            o_ref[:, cols] = (xn * cos + partner * sin).astype(BF16)

    @pl.when(n >= n_rope_tiles)
    def _():
        o_ref[...] = (acc * gain).astype(BF16)


def _qkv_projection(h, w_in, layer, col_gain, cos_tab, sin_tab, n_rope_cols):
    s, d = h.shape
    n_cols = w_in.shape[2]
    tm = min(QKV_ROWS, s)
    tn = QKV_COLS
    assert s % tm == 0 and n_cols % tn == 0 and n_rope_cols % tn == 0
    return pl.pallas_call(
        functools.partial(_qkv_kernel, n_rope_tiles=n_rope_cols // tn),
        name="qkv_proj",
        grid=(s // tm, n_cols // tn),
        in_specs=[
            pl.BlockSpec(memory_space=pl.ANY),
            pl.BlockSpec((None, d, tn), lambda i, n: (layer, 0, n)),
            pl.BlockSpec((1, tn), lambda i, n: (0, n)),
            pl.BlockSpec((tm, HEAD_DIM), lambda i, n: (i, 0)),
            pl.BlockSpec((tm, HEAD_DIM), lambda i, n: (i, 0)),
        ],
        out_specs=pl.BlockSpec((tm, tn), lambda i, n: (i, n)),
        out_shape=jax.ShapeDtypeStruct((s, n_cols), BF16),
        scratch_shapes=[pltpu.VMEM((tm, d), F32), pltpu.VMEM((tm, d), BF16), pltpu.SemaphoreType.DMA(())],
        compiler_params=_compiler_params(("arbitrary", "arbitrary")),
    )(h, w_in, col_gain, cos_tab, sin_tab)


def _transpose_bf16(x):
    return x.astype(F32).T.astype(BF16)


def _store_transposed_values(src_ref, dst_ref):
    n_blocks, _, blk = dst_ref.shape

    def body(c, carry):
        rows = pl.ds(pl.multiple_of(c * blk, blk), blk)
        dst_ref[c, :HEAD_DIM, :] = _transpose_bf16(src_ref[rows, :])
        return carry

    lax.fori_loop(0, n_blocks, body, 0, unroll=min(ATT_UNROLL, n_blocks))
    dst_ref[:, HEAD_DIM:, :] = jnp.ones((n_blocks, DENOM_ROWS, blk), BF16)


def _normalised(acc):
    return (acc[:HEAD_DIM] / acc[HEAD_DIM:HEAD_DIM + 1]).T.astype(BF16)


def _attn_a_kernel(q_ref, k_ref, v_ref, o_ref, vt_ref, s_ref, smax_ref, m_ref, acc_ref, *, tk):
    n_kblocks = k_ref.shape[0] // tk

    @pl.when(pl.program_id(1) == 0)
    def _():
        _store_transposed_values(v_ref, vt_ref)

    m_ref[...] = jnp.full_like(m_ref, -jnp.inf)
    acc_ref[...] = jnp.zeros_like(acc_ref)

    def scores(kb, slot, r):
        k = k_ref[pl.ds(pl.multiple_of(kb * tk, tk), tk), :]
        q = q_ref[:, r * HEAD_DIM:(r + 1) * HEAD_DIM]
        st = lax.dot_general(k, q, (((1,), (1,)), ((), ())), preferred_element_type=F32)
        s_ref[slot, r] = st
        smax_ref[slot, r] = jnp.max(st, axis=0, keepdims=True)

    def fold(kb, slot, r):
        m_prev = m_ref[r]
        m_new = jnp.maximum(m_prev, smax_ref[slot, r])
        p = jnp.exp2(s_ref[slot, r] - m_new)
        rescale = jnp.exp2(m_prev - m_new)
        acc_ref[r] = rescale * acc_ref[r] + jnp.dot(vt_ref[kb], p.astype(BF16), preferred_element_type=F32)
        m_ref[r] = m_new

    for r in range(GQA_REP):
        scores(0, 0, r)

    unroll = min(ATT_UNROLL, n_kblocks)

    def body(t, carry):
        for u in range(unroll):
            slot = u % 2
            kb = unroll * t + u
            nxt = jnp.minimum(kb + 1, n_kblocks - 1)
            for r in range(GQA_REP):
                scores(nxt, 1 - slot, r)
                fold(kb, slot, r)
        return carry

    lax.fori_loop(0, n_kblocks // unroll, body, 0)
    for r in range(GQA_REP):
        o_ref[:, r * HEAD_DIM:(r + 1) * HEAD_DIM] = _normalised(acc_ref[r])


def _attention_a(qkv, n_groups, k_col0, v_col0):
    s = qkv.shape[0]
    tq = min(ATT_A_Q, s)
    tk = min(ATT_A_K, s)
    assert s % tq == 0 and (s // tk) % min(ATT_UNROLL, s // tk) == 0
    group_w = GQA_REP * HEAD_DIM
    k_blk0 = k_col0 // HEAD_DIM
    v_blk0 = v_col0 // HEAD_DIM
    return pl.pallas_call(
        functools.partial(_attn_a_kernel, tk=tk),
        name="attn_a",
        grid=(n_groups, s // tq),
        in_specs=[
            pl.BlockSpec((tq, group_w), lambda g, i: (i, g)),
            pl.BlockSpec((s, HEAD_DIM), lambda g, i: (0, k_blk0 + g)),
            pl.BlockSpec((s, HEAD_DIM), lambda g, i: (0, v_blk0 + g)),
        ],
        out_specs=pl.BlockSpec((tq, group_w), lambda g, i: (i, g)),
        out_shape=jax.ShapeDtypeStruct((s, n_groups * group_w), BF16),
        scratch_shapes=[pltpu.VMEM((s // tk, HEAD_DIM + DENOM_ROWS, tk), BF16),
                        pltpu.VMEM((2, GQA_REP, tk, tq), F32),
                        pltpu.VMEM((2, GQA_REP, 1, tq), F32),
                        pltpu.VMEM((GQA_REP, 1, tq), F32),
                        pltpu.VMEM((GQA_REP, HEAD_DIM + DENOM_ROWS, tq), F32)],
        compiler_params=_compiler_params(("parallel", "arbitrary")),
    )(qkv, qkv, qkv)


def _attn_b_kernel(q_ref, k_ref, v_ref, rev_ref, o_ref, vt_ref, t_ref, s_ref, smax_ref, *, blk, n_side):
    n_blocks = k_ref.shape[0] // blk
    n_tiles = 2 * n_side + 1
    _store_transposed_values(v_ref, vt_ref)

    for jj in range(n_tiles):
        start = (n_tiles - 1 - jj) * blk
        window = jnp.broadcast_to(rev_ref[0, :, start:start + 2 * blk], (blk, 2 * blk))
        t_ref[jj] = pltpu.roll(window, 0, 1, stride=1, stride_axis=0)[:, blk:]
    t_ref[n_tiles] = jnp.full((blk, blk), MASKED_LOGIT, F32)

    def band(i, jj):
        kb = i + (jj - n_side)
        valid = jnp.logical_and(kb >= 0, kb < n_blocks)
        return jnp.clip(kb, 0, n_blocks - 1), jnp.where(valid, jj, n_tiles)

    def q_block(i):
        return q_ref[pl.ds(pl.multiple_of(i * blk, blk), blk), :]

    def scores_tile(i, slot, jj, qt, col_max):
        kb, tile = band(i, jj)
        k = k_ref[pl.ds(pl.multiple_of(kb * blk, blk), blk), :]
        s = lax.dot_general(k, qt, (((1,), (1,)), ((), ())), preferred_element_type=F32) + t_ref[tile]
        s_ref[slot, jj] = s
        tile_max = jnp.max(s, axis=0, keepdims=True)
        return tile_max if col_max is None else jnp.maximum(col_max, tile_max)

    def fold_tile(i, slot, jj, m, acc):
        kb, _ = band(i, jj)
        p = jnp.exp2(s_ref[slot, jj] - m)
        return acc + jnp.dot(vt_ref[kb], p.astype(BF16), preferred_element_type=F32)

    qt = q_block(0)
    col_max = None
    for jj in range(n_tiles):
        col_max = scores_tile(0, 0, jj, qt, col_max)
    smax_ref[0] = col_max

    unroll = min(ATT_UNROLL, n_blocks)

    def body(t, carry):
        for u in range(unroll):
            slot = u % 2
            i = unroll * t + u
            nxt = jnp.minimum(i + 1, n_blocks - 1)
            qt = q_block(nxt)
            m = smax_ref[slot]
            acc = jnp.zeros((HEAD_DIM + DENOM_ROWS, blk), F32)
            col_max = None
            for jj in range(n_tiles):
                col_max = scores_tile(nxt, 1 - slot, jj, qt, col_max)
                acc = fold_tile(i, slot, jj, m, acc)
            smax_ref[1 - slot] = col_max
            o_ref[pl.ds(pl.multiple_of(i * blk, blk), blk), :] = _normalised(acc)
        return carry

    lax.fori_loop(0, n_blocks // unroll, body, 0)


def _attention_b(qkv, offset_terms, n_heads, q_col0, k_col0, v_col0):
    s = qkv.shape[0]
    blk = ATT_B_BLK
    n_side = ATT_B_HALO // blk
    n_tiles = 2 * n_side + 1
    assert (s // blk) % min(ATT_UNROLL, s // blk) == 0 and ATT_B_HALO % blk == 0
    q_blk0, k_blk0, v_blk0 = (c // HEAD_DIM for c in (q_col0, k_col0, v_col0))
    return pl.pallas_call(
        functools.partial(_attn_b_kernel, blk=blk, n_side=n_side),
        name="attn_b",
        grid=(n_heads,),
        in_specs=[
            pl.BlockSpec((s, HEAD_DIM), lambda h: (0, q_blk0 + h)),
            pl.BlockSpec((s, HEAD_DIM), lambda h: (0, k_blk0 + h)),
            pl.BlockSpec((s, HEAD_DIM), lambda h: (0, v_blk0 + h)),
            pl.BlockSpec((1, 1, (n_tiles + 1) * blk), lambda h: (h, 0, 0)),
        ],
        out_specs=pl.BlockSpec((s, HEAD_DIM), lambda h: (0, h)),
        out_shape=jax.ShapeDtypeStruct((s, n_heads * HEAD_DIM), BF16),
        scratch_shapes=[pltpu.VMEM((s // blk, HEAD_DIM + DENOM_ROWS, blk), BF16),
                        pltpu.VMEM((n_tiles + 1, blk, blk), F32),
                        pltpu.VMEM((2, n_tiles, blk, blk), F32),
                        pltpu.VMEM((2, 1, blk), F32)],
        compiler_params=_compiler_params(("parallel",)),
    )(qkv, qkv, qkv, offset_terms)


def _out_proj_ln_kernel(a_ref, b_ref, w_ref, h_ref, g_ref, beta_ref, o_ref, *, alpha, n_a_chunks):
    k = pl.program_id(1)

    @pl.when(k == 0)
    def _():
        o_ref[...] = alpha * h_ref[...]

    @pl.when(k < n_a_chunks)
    def _():
        o_ref[...] += jnp.dot(a_ref[...], w_ref[...], preferred_element_type=F32)

    @pl.when(k >= n_a_chunks)
    def _():
        o_ref[...] += jnp.dot(b_ref[...], w_ref[...], preferred_element_type=F32)

    @pl.when(k == pl.num_programs(1) - 1)
    def _():
        _layer_norm_in_place(o_ref, g_ref, beta_ref)


def _out_proj_ln(out_a, out_b, w_out, h, ln_g, ln_b, alpha):
    s, d = h.shape
    wa, wb = out_a.shape[1], out_b.shape[1]
    tm = min(OUT_ROWS, s)
    tk = OUT_K
    assert s % tm == 0 and wa % tk == 0 and wb % tk == 0 and tm % LN_CHUNK == 0
    na, nb = wa // tk, wb // tk
    return pl.pallas_call(
        functools.partial(_out_proj_ln_kernel, alpha=alpha, n_a_chunks=na),
        name="out_proj_ln",
        grid=(s // tm, na + nb),
        in_specs=[
            pl.BlockSpec((tm, tk), lambda i, k: (i, jnp.minimum(k, na - 1))),
            pl.BlockSpec((tm, tk), lambda i, k: (i, jnp.maximum(k - na, 0))),
            pl.BlockSpec((tk, d), lambda i, k: (k, 0)),
            pl.BlockSpec((tm, d), lambda i, k: (i, 0)),
            pl.BlockSpec((1, d), lambda i, k: (0, 0)),
            pl.BlockSpec((1, d), lambda i, k: (0, 0)),
        ],
        out_specs=pl.BlockSpec((tm, d), lambda i, k: (i, 0)),
        out_shape=jax.ShapeDtypeStruct((s, d), F32),
        compiler_params=_compiler_params(("parallel", "arbitrary")),
    )(out_a, out_b, w_out, h, ln_g, ln_b)


def _rope_tables(seq_len):
    rows = seq_len // GRID_W
    row = jnp.repeat(jnp.arange(rows, dtype=F32), GRID_W)
    col = jnp.tile(jnp.arange(GRID_W, dtype=F32), rows)
    half = HEAD_DIM // 2
    inv_freq = ROPE_THETA ** (-jnp.arange(0, half, 2, dtype=F32) / half)
    ang = jnp.concatenate([row[:, None] * inv_freq, col[:, None] * inv_freq], axis=-1)
    cos = jnp.repeat(jnp.cos(ang), 2, axis=-1)
    sign = jnp.tile(jnp.array([-1.0, 1.0], F32), HEAD_DIM // 2)
    sin = jnp.repeat(jnp.sin(ang), 2, axis=-1) * sign
    return cos, sin


def _t5_bucket(rel):
    nb = N_REL_BUCKETS // 2
    max_exact = nb // 2
    ret = jnp.where(rel > 0, nb, 0)
    n = jnp.abs(rel)
    large = max_exact + (jnp.log(jnp.maximum(n, 1).astype(F32) / max_exact)
                         / math.log(REL_MAX_DIST / max_exact) * (nb - max_exact)).astype(jnp.int32)
    large = jnp.minimum(large, nb - 1)
    return ret + jnp.where(n < max_exact, n, large)


def _dilated_offset_terms(rel_bias, blk):
    reach = ATT_B_HALO + blk
    off = np.arange(-reach + 1, reach)
    mult = np.zeros(off.shape, np.float64)
    for win, dil in DILATED_CONFIGS:
        mult += (np.abs(off) <= win // 2) & (off % dil == 0)
    log_mult = jnp.asarray(np.log(np.maximum(mult, 1.0)), F32)
    present = jnp.asarray(mult > 0)
    bias = rel_bias[_t5_bucket(jnp.asarray(off, jnp.int32))].astype(F32)
    term = jnp.where(present[:, None], (bias + log_mult[:, None]) * LOG2_E, MASKED_LOGIT).T
    n_heads = term.shape[0]
    return jnp.concatenate([jnp.zeros((n_heads, 1), F32), term[:, ::-1]], axis=1)[:, None, :]


def kernel(x, ffn1_w_gate, ffn1_w_up, ffn1_w_down, ln1_g, ln1_b, w_in, q_norm_g, k_norm_g, rel_bias, w_out,
           ln2_g, ln2_b, ffn2_w_gate, ffn2_w_up, ffn2_w_down, ln3_g, ln3_b):
    batch, s, d = x.shape
    depth = ffn1_w_gate.shape[0]
    assert batch == 1
    n_heads = d // HEAD_DIM
    heads_a = n_heads // 2
    kv_a = heads_a // GQA_REP
    heads_b = n_heads - heads_a
    q_a, kv_a_w, qkv_b = heads_a * HEAD_DIM, kv_a * HEAD_DIM, heads_b * HEAD_DIM
    col_ka, col_va = q_a, q_a + kv_a_w
    col_qb = q_a + 2 * kv_a_w
    col_kb, col_vb = col_qb + qkv_b, col_qb + 2 * qkv_b
    alpha = (2.0 * depth) ** 0.25
    scale = HEAD_DIM ** -0.5 * LOG2_E

    cos_tab, sin_tab = _rope_tables(s)
    offset_terms = _dilated_offset_terms(rel_bias, ATT_B_BLK)

    h = x[0]
    for l in range(depth):
        h = _ffn_ln(h, ffn1_w_gate, ffn1_w_up, ffn1_w_down, l, ln1_g[l][None], ln1_b[l][None], alpha)
        col_gain = jnp.concatenate([
            jnp.tile(q_norm_g[l], heads_a) * scale, jnp.tile(k_norm_g[l], kv_a), jnp.ones((kv_a_w,), F32),
            jnp.full((qkv_b,), scale, F32), jnp.ones((2 * qkv_b,), F32)])[None]
        qkv = _qkv_projection(h, w_in, l, col_gain, cos_tab, sin_tab, col_va)
        out_a = _attention_a(qkv, kv_a, col_ka, col_va)
        out_b = _attention_b(qkv, offset_terms, heads_b, col_qb, col_kb, col_vb)
        h = _out_proj_ln(out_a, out_b, w_out[l].astype(BF16), h, ln2_g[l][None], ln2_b[l][None], alpha)
        h = _ffn_ln(h, ffn2_w_gate, ffn2_w_up, ffn2_w_down, l, ln3_g[l][None], ln3_b[l][None], alpha)
    return h[None]
```
